```python
import math
import jax, jax.numpy as jnp
from jax import lax
import numpy as np

D_MODEL = 2048
BATCH = 1
SEQ = 8192
DEPTH = 1

MLA_HEADS = 16
QK_NOPE_DIM = 128
QK_ROPE_DIM = 64
V_HEAD_DIM = 128
Q_LORA_RANK = 768
KV_LORA_RANK = 512
ROPE_THETA = 10000.0
Q_BLOCK = 128
MLA_WIDTH = MLA_HEADS * V_HEAD_DIM

SGU_GROUPS = 16
SGU_CHUNK = 128
SGU_WIDTH = D_MODEL
SGU_GROUP_DIM = SGU_WIDTH // SGU_GROUPS

NORM_EPS = 1e-6

IN_SPLITS = (Q_LORA_RANK, KV_LORA_RANK, QK_ROPE_DIM, MLA_WIDTH,
             SGU_WIDTH, SGU_WIDTH, SGU_WIDTH, D_MODEL, D_MODEL)
IN_WIDTH = sum(IN_SPLITS)

kernel_name = "hybrid_mla_sgu_gated_block"


def rmsnorm(x, g):
    xf = x.astype(jnp.float32)
    y = xf * lax.rsqrt(jnp.mean(xf * xf, axis=-1, keepdims=True) + NORM_EPS)
    return (y * g.astype(jnp.float32)).astype(x.dtype)


def split_cols(t, sizes):
    idx = list(np.cumsum(sizes)[:-1])
    return jnp.split(t, idx, axis=-1)


def rope_tables(positions, dtype):
    inv_freq = 1.0 / (ROPE_THETA ** (jnp.arange(0, QK_ROPE_DIM, 2, dtype=jnp.float32) / QK_ROPE_DIM))
    ang = positions.astype(jnp.float32)[..., None] * inv_freq
    return jnp.cos(ang).astype(dtype), jnp.sin(ang).astype(dtype)


def apply_rope(t, cos, sin):
    t1, t2 = jnp.split(t, 2, axis=-1)
    return jnp.concatenate([t1 * cos - t2 * sin, t2 * cos + t1 * sin], axis=-1)


def causal_mla_attention(q_nope, q_rope, k_nope, k_rope, v):
    B, S, H, _ = q_nope.shape
    nb = S // Q_BLOCK
    scale = 1.0 / math.sqrt(QK_NOPE_DIM + QK_ROPE_DIM)
    qn = q_nope.reshape(B, nb, Q_BLOCK, H, QK_NOPE_DIM).transpose(1, 0, 2, 3, 4)
    qr = q_rope.reshape(B, nb, Q_BLOCK, H, QK_ROPE_DIM).transpose(1, 0, 2, 3, 4)
    kpos = jnp.arange(S)

    def one_block(args):
        qn_b, qr_b, blk = args
        s = (jnp.einsum('bqhd,bkhd->bhqk', qn_b, k_nope)
             + jnp.einsum('bqhd,bkd->bhqk', qr_b, k_rope))
        s = s.astype(jnp.float32) * scale
        qpos = blk * Q_BLOCK + jnp.arange(Q_BLOCK)
        mask = kpos[None, :] <= qpos[:, None]
        s = jnp.where(mask[None, None], s, -jnp.inf)
        p = jax.nn.softmax(s, axis=-1).astype(v.dtype)
        return jnp.einsum('bhqk,bkhd->bqhd', p, v)

    o = lax.map(one_block, (qn, qr, jnp.arange(nb)))
    return o.transpose(1, 0, 2, 3, 4).reshape(B, S, H * V_HEAD_DIM)


def chunked_causal_sgu(u, v, sgu_norm_g, w_spatial, b_spatial):
    B, S, _ = v.shape
    nc = S // SGU_CHUNK
    v = rmsnorm(v, sgu_norm_g)
    vc = v.reshape(B, nc, SGU_CHUNK, SGU_GROUPS, SGU_GROUP_DIM)
    tril = jnp.tril(jnp.ones((SGU_CHUNK, SGU_CHUNK), dtype=w_spatial.dtype))
    ws = w_spatial * tril[None]
    mixed = jnp.einsum('gts,bnsgd->bntgd', ws, vc) + b_spatial.T[None, None, :, :, None]
    return u * mixed.reshape(B, S, SGU_WIDTH)


def setup_inputs(seed: int = 0) -> dict:
    key = jax.random.key(seed)
    ks = jax.random.split(key, 16)
    L, D = DEPTH, D_MODEL
    f32 = jnp.float32

    def nrm(k, shape, fan_in, mult=1.0):
        return jax.random.normal(k, shape, f32) * (mult * fan_in ** -0.5)

    def gain(k, shape):
        return 1.0 + 0.01 * jax.random.normal(k, shape, f32)

    x = jax.random.normal(ks[0], (BATCH, SEQ, D), f32)
    c = jax.random.normal(ks[1], (BATCH, D), f32)
    positions = jnp.broadcast_to(jnp.arange(SEQ, dtype=jnp.int32)[None], (BATCH, SEQ))
    return {
        "x": x,
        "c": c,
        "positions": positions,
        "attn_norm_g": gain(ks[2], (L, D)),
        "w_ada": nrm(ks[3], (L, D, 3 * D), D, 0.1),
        "b_ada": 0.01 * jax.random.normal(ks[4], (L, 3 * D), f32),
        "w_in": nrm(ks[5], (L, D, IN_WIDTH), D),
        "q_norm_g": gain(ks[6], (L, Q_LORA_RANK)),
        "w_uq": nrm(ks[7], (L, Q_LORA_RANK, MLA_HEADS * (QK_NOPE_DIM + QK_ROPE_DIM)), Q_LORA_RANK),
        "kv_norm_g": gain(ks[8], (L, KV_LORA_RANK)),
        "w_ukv": nrm(ks[9], (L, KV_LORA_RANK, MLA_HEADS * (QK_NOPE_DIM + V_HEAD_DIM)), KV_LORA_RANK),
        "sgu_norm_g": gain(ks[10], (L, SGU_WIDTH)),
        "w_spatial": nrm(ks[11], (L, SGU_GROUPS, SGU_CHUNK, SGU_CHUNK), SGU_CHUNK),
        "b_spatial": 1.0 + 0.02 * jax.random.normal(ks[12], (L, SGU_GROUPS, SGU_CHUNK), f32),
        "w_out": nrm(ks[13], (L, D, D), D),
        "final_norm_g": gain(ks[14], (D,)),
    }


def reference(x, c, positions, attn_norm_g, w_ada, b_ada, w_in, q_norm_g, w_uq, kv_norm_g,
              w_ukv, sgu_norm_g, w_spatial, b_spatial, w_out, final_norm_g):
    B, S, D = x.shape
    cos, sin = rope_tables(positions, x.dtype)
    c_act = jax.nn.silu(c)
    for l in range(DEPTH):
        mod = c_act @ w_ada[l] + b_ada[l]
        shift, scale, gate = jnp.split(mod, 3, axis=-1)
        h = rmsnorm(x, attn_norm_g[l]) * (1.0 + scale[:, None]) + shift[:, None]

        proj = h @ w_in[l]
        q_lat, kv_lat, k_rope, z_mla, u, v, z_sgu, g_mla, g_sgu = split_cols(proj, IN_SPLITS)

        q = (rmsnorm(q_lat, q_norm_g[l]) @ w_uq[l]).reshape(B, S, MLA_HEADS, QK_NOPE_DIM + QK_ROPE_DIM)
        q_nope, q_rope = q[..., :QK_NOPE_DIM], q[..., QK_NOPE_DIM:]
        q_rope = apply_rope(q_rope, cos[:, :, None, :], sin[:, :, None, :])
        kv = (rmsnorm(kv_lat, kv_norm_g[l]) @ w_ukv[l]).reshape(B, S, MLA_HEADS, QK_NOPE_DIM + V_HEAD_DIM)
        k_nope, v_mla = kv[..., :QK_NOPE_DIM], kv[..., QK_NOPE_DIM:]
        k_rope = apply_rope(k_rope, cos, sin)
        y_mla = causal_mla_attention(q_nope, q_rope, k_nope, k_rope, v_mla) * jax.nn.silu(z_mla)

        y_sgu = chunked_causal_sgu(u, v, sgu_norm_g[l], w_spatial[l], b_spatial[l]) * jax.nn.silu(z_sgu)

        merged = jax.nn.sigmoid(g_mla) * y_mla + jax.nn.sigmoid(g_sgu) * y_sgu
        x = x + gate[:, None] * (merged @ w_out[l])
    return rmsnorm(x, final_norm_g)
```

```python
import functools
import math

import jax
import jax.numpy as jnp
from jax import lax
from jax.experimental import pallas as pl
from jax.experimental.pallas import tpu as pltpu

D_MODEL = 2048
SEQ = 8192
HEADS = 16
NOPE = 128
ROPE = 64
VDIM = 128
Q_RANK = 768
KV_RANK = 512
ROPE_THETA = 10000.0
GROUPS = 16
CHUNK = 128
EPS = 1e-6

LANE = 128
HEAD_SLAB = 2 * LANE
LAT_WIDTH = 1536
BIG_WIDTH = 6 * D_MODEL
VMEM_LIMIT = 56 * 1024 * 1024

Q_SCALE = (1.0 / math.sqrt(NOPE + ROPE)) * math.log2(math.e)

BF16 = jnp.bfloat16
F32 = jnp.float32


def _sigmoid(x):
    return 1.0 / (1.0 + jnp.exp(-x))


def _rms(x, g):
    ms = jnp.mean(x * x, axis=-1, keepdims=True)
    return x * lax.rsqrt(ms + EPS) * g


def _adaln_kernel(c_ref, w_ref, b_ref, o_ref):
    c = c_ref[...]
    c_act = c * _sigmoid(c)
    o_ref[...] = jnp.dot(c_act.astype(BF16), w_ref[...].astype(BF16),
                         preferred_element_type=F32) + b_ref[...]


def _adaln(c8, w_ada, b_ada):
    tn = 1536
    n = w_ada.shape[1]
    return pl.pallas_call(
        _adaln_kernel,
        grid=(n // tn,),
        in_specs=[pl.BlockSpec((8, D_MODEL), lambda j: (0, 0)),
                  pl.BlockSpec((D_MODEL, tn), lambda j: (0, j)),
                  pl.BlockSpec((1, tn), lambda j: (0, j))],
        out_specs=pl.BlockSpec((8, tn), lambda j: (0, j)),
        out_shape=jax.ShapeDtypeStruct((8, n), F32),
        compiler_params=pltpu.CompilerParams(
            dimension_semantics=("arbitrary",), vmem_limit_bytes=VMEM_LIMIT),
        name="adaln",
    )(c8, w_ada, b_ada)


def _inproj_kernel(x_ref, g_ref, scale_ref, shift_ref, w_ref, o_ref, h_ref):
    @pl.when(pl.program_id(1) == 0)
    def _():
        y = _rms(x_ref[...], g_ref[...])
        h_ref[...] = (y * (1.0 + scale_ref[...]) + shift_ref[...]).astype(BF16)

    o_ref[...] = jnp.dot(h_ref[...], w_ref[...], preferred_element_type=F32).astype(o_ref.dtype)


def _inproj(x2, g, scale, shift, w_cat):
    tm, tn = 1024, 1536
    s, n = x2.shape[0], w_cat.shape[1]
    vec = pl.BlockSpec((1, D_MODEL), lambda i, j: (0, 0))
    return pl.pallas_call(
        _inproj_kernel,
        grid=(s // tm, n // tn),
        in_specs=[pl.BlockSpec((tm, D_MODEL), lambda i, j: (i, 0)), vec, vec, vec,
                  pl.BlockSpec((D_MODEL, tn), lambda i, j: (0, j))],
        out_specs=pl.BlockSpec((tm, tn), lambda i, j: (i, j)),
        out_shape=jax.ShapeDtypeStruct((s, n), BF16),
        scratch_shapes=[pltpu.VMEM((tm, D_MODEL), BF16)],
        compiler_params=pltpu.CompilerParams(
            dimension_semantics=("parallel", "arbitrary"), vmem_limit_bytes=VMEM_LIMIT),
        name="inproj",
    )(x2, g, scale, shift, w_cat)


def _rope_slab(t, cos_t, sin_a, sin_b):
    return t * cos_t + pltpu.roll(t, 96, 1) * sin_a + pltpu.roll(t, 32, 1) * sin_b


def _uproj_kernel(lat_ref, qg_ref, kvg_ref, wq_ref, wk_ref, wv_ref, cos_ref, sa_ref, sb_ref,
                  q_ref, k_ref, v_ref):
    lat = lat_ref[...].astype(F32)
    cos_t, sin_a, sin_b = cos_ref[...], sa_ref[...], sb_ref[...]

    qn = _rms(lat[:, :Q_RANK], qg_ref[...]).astype(BF16)
    q = jnp.dot(qn, wq_ref[...], preferred_element_type=F32)
    for h in range(HEADS):
        lo = h * HEAD_SLAB
        q_ref[:, lo:lo + LANE] = (q[:, lo:lo + LANE] * Q_SCALE).astype(BF16)
        rot = _rope_slab(q[:, lo + LANE:lo + HEAD_SLAB], cos_t, sin_a, sin_b)
        q_ref[:, lo + LANE:lo + HEAD_SLAB] = (rot * Q_SCALE).astype(BF16)

    kvn = _rms(lat[:, Q_RANK:Q_RANK + KV_RANK], kvg_ref[...]).astype(BF16)
    v_ref[...] = jnp.dot(kvn, wv_ref[...], preferred_element_type=F32).astype(BF16)
    kn = jnp.dot(kvn, wk_ref[...], preferred_element_type=F32)
    kr = _rope_slab(lat[:, Q_RANK + KV_RANK:Q_RANK + KV_RANK + LANE],
                    cos_t, sin_a, sin_b).astype(BF16)
    for h in range(HEADS):
        lo = h * HEAD_SLAB
        k_ref[:, lo:lo + LANE] = kn[:, h * LANE:(h + 1) * LANE].astype(BF16)
        k_ref[:, lo + LANE:lo + HEAD_SLAB] = kr


def _uproj(proj, qg, kvg, wq, wk, wv, cos_t, sin_a, sin_b):
    tm = 512
    s = proj.shape[0]
    lat_block = BIG_WIDTH // LAT_WIDTH
    const = lambda i: (0, 0)
    row = lambda i: (i, 0)
    return pl.pallas_call(
        _uproj_kernel,
        grid=(s // tm,),
        in_specs=[pl.BlockSpec((tm, LAT_WIDTH), lambda i: (i, lat_block)),
                  pl.BlockSpec((1, Q_RANK), const), pl.BlockSpec((1, KV_RANK), const),
                  pl.BlockSpec(wq.shape, const), pl.BlockSpec(wk.shape, const),
                  pl.BlockSpec(wv.shape, const),
                  pl.BlockSpec((tm, LANE), row), pl.BlockSpec((tm, LANE), row),
                  pl.BlockSpec((tm, LANE), row)],
        out_specs=[pl.BlockSpec((tm, HEADS * HEAD_SLAB), row),
                   pl.BlockSpec((tm, HEADS * HEAD_SLAB), row),
                   pl.BlockSpec((tm, HEADS * VDIM), row)],
        out_shape=[jax.ShapeDtypeStruct((s, HEADS * HEAD_SLAB), BF16),
                   jax.ShapeDtypeStruct((s, HEADS * HEAD_SLAB), BF16),
                   jax.ShapeDtypeStruct((s, HEADS * VDIM), BF16)],
        compiler_params=pltpu.CompilerParams(
            dimension_semantics=("parallel",), vmem_limit_bytes=VMEM_LIMIT),
        name="uproj",
    )(proj, qg, kvg, wq, wk, wv, cos_t, sin_a, sin_b)


def _flash_kernel(q_ref, k_ref, v_ref, o_ref, *, tq):
    i = pl.program_id(1)
    q = q_ref[...]

    def block(j, carry, masked):
        m, l, acc = carry
        start = pl.multiple_of(j * tq, tq)
        k = k_ref[pl.ds(start, tq), :]
        v = v_ref[pl.ds(start, tq), :]
        s = lax.dot_general(q, k, (((1,), (1,)), ((), ())), preferred_element_type=F32)
        if masked:
            rows = lax.broadcasted_iota(jnp.int32, (tq, tq), 0)
            cols = lax.broadcasted_iota(jnp.int32, (tq, tq), 1)
            s = jnp.where(cols <= rows, s, -jnp.inf)
        m_new = jnp.maximum(m, jnp.max(s, axis=-1, keepdims=True))
        alpha = jnp.exp2(m - m_new)
        p = jnp.exp2(s - m_new)
        l_new = alpha * l + jnp.sum(p, axis=-1, keepdims=True)
        acc_new = alpha * acc + jnp.dot(p.astype(BF16), v, preferred_element_type=F32)
        return m_new, l_new, acc_new

    init = (jnp.full((tq, 1), -jnp.inf, F32), jnp.zeros((tq, 1), F32),
            jnp.zeros((tq, VDIM), F32))
    carry = lax.fori_loop(0, i, functools.partial(block, masked=False), init)
    _, l, acc = block(i, carry, True)
    o_ref[...] = (acc / l).astype(o_ref.dtype)


def _flash(q, k, v):
    tq = 512
    s = q.shape[0]
    return pl.pallas_call(
        functools.partial(_flash_kernel, tq=tq),
        grid=(HEADS, s // tq),
        in_specs=[pl.BlockSpec((tq, HEAD_SLAB), lambda h, i: (i, h)),
                  pl.BlockSpec((s, HEAD_SLAB), lambda h, i: (0, h)),
                  pl.BlockSpec((s, VDIM), lambda h, i: (0, h))],
        out_specs=pl.BlockSpec((tq, VDIM), lambda h, i: (i, h)),
        out_shape=jax.ShapeDtypeStruct((s, HEADS * VDIM), BF16),
        compiler_params=pltpu.CompilerParams(
            dimension_semantics=("parallel", "arbitrary"), vmem_limit_bytes=VMEM_LIMIT),
        name="flash",
    )(q, k, v)


def _out_kernel(a_ref, zm_ref, u_ref, v_ref, zs_ref, gm_ref, gs_ref, x_ref, gate_ref,
                sg_ref, ws_ref, bs_ref, wo_ref, fg_ref, o_ref, mix_ref, *, tm):
    rows = lax.broadcasted_iota(jnp.int32, (CHUNK, CHUNK), 0)
    cols = lax.broadcasted_iota(jnp.int32, (CHUNK, CHUNK), 1)
    causal = cols <= rows

    vn = _rms(v_ref[...].astype(F32), sg_ref[...]).astype(BF16)
    for g in range(GROUPS):
        w = jnp.where(causal, ws_ref[g], 0.0).astype(BF16)
        for c in range(tm // CHUNK):
            r = slice(c * CHUNK, (c + 1) * CHUNK)
            col = slice(g * CHUNK, (g + 1) * CHUNK)
            mix_ref[r, col] = jnp.dot(w, vn[r, col], preferred_element_type=F32) + bs_ref[:, col]

    zm = zm_ref[...].astype(F32)
    zs = zs_ref[...].astype(F32)
    y_mla = a_ref[...].astype(F32) * (zm * _sigmoid(zm))
    y_sgu = u_ref[...].astype(F32) * mix_ref[...] * (zs * _sigmoid(zs))
    merged = (_sigmoid(gm_ref[...].astype(F32)) * y_mla
              + _sigmoid(gs_ref[...].astype(F32)) * y_sgu)
    y = jnp.dot(merged.astype(BF16), wo_ref[...], preferred_element_type=F32)
    o_ref[...] = _rms(x_ref[...] + gate_ref[...] * y, fg_ref[...])


def _outproj(attn, proj, x2, gate, sg, ws, bs_full, wo, fg):
    tm = 256
    s = x2.shape[0]
    row = lambda i: (i, 0)
    const = lambda i: (0, 0)
    big = lambda k: pl.BlockSpec((tm, D_MODEL), lambda i, k=k: (i, k))
    vec = pl.BlockSpec((1, D_MODEL), const)
    return pl.pallas_call(
        functools.partial(_out_kernel, tm=tm),
        grid=(s // tm,),
        in_specs=[pl.BlockSpec((tm, D_MODEL), row),
                  big(0), big(1), big(2), big(3), big(4), big(5),
                  pl.BlockSpec((tm, D_MODEL), row), vec, vec,
                  pl.BlockSpec(ws.shape, lambda i: (0, 0, 0)),
                  pl.BlockSpec(bs_full.shape, const),
                  pl.BlockSpec(wo.shape, const), vec],
        out_specs=pl.BlockSpec((tm, D_MODEL), row),
        out_shape=jax.ShapeDtypeStruct((s, D_MODEL), F32),
        scratch_shapes=[pltpu.VMEM((tm, D_MODEL), F32)],
        compiler_params=pltpu.CompilerParams(
            dimension_semantics=("parallel",), vmem_limit_bytes=VMEM_LIMIT),
        name="outproj",
    )(attn, proj, proj, proj, proj, proj, proj, x2, gate, sg, ws, bs_full, wo, fg)


def _rope_tables(positions):
    inv_freq = 1.0 / (ROPE_THETA ** (jnp.arange(0, ROPE, 2, dtype=F32) / ROPE))
    ang = positions.astype(F32)[:, None] * inv_freq
    cos, sin = jnp.cos(ang), jnp.sin(ang)
    z = jnp.zeros_like(cos)
    cos_t = jnp.concatenate([cos, cos, z, z], axis=-1)
    sin_a = jnp.concatenate([-sin, z, z, z], axis=-1)
    sin_b = jnp.concatenate([z, sin, z, z], axis=-1)
    return cos_t, sin_a, sin_b


def _layer(x2, c8, positions, attn_norm_g, w_ada, b_ada, w_in, q_norm_g, w_uq, kv_norm_g,
           w_ukv, sgu_norm_g, w_spatial, b_spatial, w_out):
    d = D_MODEL
    mod = _adaln(c8, w_ada, b_ada.reshape(1, 3 * d))[0:1]
    shift, scale, gate = mod[:, :d], mod[:, d:2 * d], mod[:, 2 * d:]

    n_lat = Q_RANK + KV_RANK + ROPE
    w_cat = jnp.concatenate(
        [w_in[:, n_lat:], w_in[:, :n_lat], jnp.zeros((d, LAT_WIDTH - n_lat), w_in.dtype)],
        axis=1).astype(BF16)
    proj = _inproj(x2, attn_norm_g.reshape(1, d), scale, shift, w_cat)

    wq = w_uq.reshape(Q_RANK, HEADS, NOPE + ROPE)
    wq = jnp.pad(wq, ((0, 0), (0, 0), (0, HEAD_SLAB - NOPE - ROPE)))
    wq = wq.reshape(Q_RANK, HEADS * HEAD_SLAB).astype(BF16)
    wkv = w_ukv.reshape(KV_RANK, HEADS, NOPE + VDIM)
    wk = wkv[:, :, :NOPE].reshape(KV_RANK, HEADS * NOPE).astype(BF16)
    wv = wkv[:, :, NOPE:].reshape(KV_RANK, HEADS * VDIM).astype(BF16)
    cos_t, sin_a, sin_b = _rope_tables(positions)
    q, k, v = _uproj(proj, q_norm_g.reshape(1, Q_RANK), kv_norm_g.reshape(1, KV_RANK),
                     wq, wk, wv, cos_t, sin_a, sin_b)

    attn = _flash(q, k, v)

    bs_full = jnp.repeat(b_spatial.T, CHUNK, axis=1)
    return attn, proj, gate, bs_full


def kernel(x, c, positions, attn_norm_g, w_ada, b_ada, w_in, q_norm_g, w_uq, kv_norm_g, w_ukv,
           sgu_norm_g, w_spatial, b_spatial, w_out, final_norm_g):
    b, s, d = x.shape
    assert (b, s, d) == (1, SEQ, D_MODEL) and attn_norm_g.shape[0] == 1
    x2 = x.reshape(s, d)
    c8 = jnp.broadcast_to(c, (8, d))
    attn, proj, gate, bs_full = _layer(
        x2, c8, positions[0], attn_norm_g[0], w_ada[0], b_ada[0], w_in[0], q_norm_g[0],
        w_uq[0], kv_norm_g[0], w_ukv[0], sgu_norm_g[0], w_spatial[0], b_spatial[0], w_out[0])
    out = _outproj(attn, proj, x2, gate, sgu_norm_g[0].reshape(1, d), w_spatial[0], bs_full,
                   w_out[0].astype(BF16), final_norm_g.reshape(1, d))
    return out.reshape(b, s, d)
```

```python
import functools
import math

import jax
import jax.numpy as jnp
from jax import lax
from jax.experimental import pallas as pl
from jax.experimental.pallas import tpu as pltpu

D_MODEL = 2048
SEQ = 8192
HEADS = 16
NOPE = 128
ROPE = 64
VDIM = 128
Q_RANK = 768
KV_RANK = 512
ROPE_THETA = 10000.0
GROUPS = 16
CHUNK = 128
EPS = 1e-6

LANE = 128
HEAD_SLAB = 2 * LANE
LAT_WIDTH = 1536
BIG_WIDTH = 6 * D_MODEL
VMEM_LIMIT = 56 * 1024 * 1024

Q_SCALE = (1.0 / math.sqrt(NOPE + ROPE)) * math.log2(math.e)

BF16 = jnp.bfloat16
F32 = jnp.float32


def _sigmoid(x):
    return 0.5 * jnp.tanh(0.5 * x) + 0.5


def _rms(x, g):
    ms = jnp.mean(x * x, axis=-1, keepdims=True)
    return x * lax.rsqrt(ms + EPS) * g


def _adaln_kernel(c_ref, w_ref, b_ref, o_ref):
    c = c_ref[...]
    c_act = c * _sigmoid(c)
    o_ref[...] = jnp.dot(c_act.astype(BF16), w_ref[...].astype(BF16),
                         preferred_element_type=F32) + b_ref[...]


def _adaln(c8, w_ada, b_ada):
    tn = 1536
    n = w_ada.shape[1]
    return pl.pallas_call(
        _adaln_kernel,
        grid=(n // tn,),
        in_specs=[pl.BlockSpec((8, D_MODEL), lambda j: (0, 0)),
                  pl.BlockSpec((D_MODEL, tn), lambda j: (0, j)),
                  pl.BlockSpec((1, tn), lambda j: (0, j))],
        out_specs=pl.BlockSpec((8, tn), lambda j: (0, j)),
        out_shape=jax.ShapeDtypeStruct((8, n), F32),
        compiler_params=pltpu.CompilerParams(
            dimension_semantics=("arbitrary",), vmem_limit_bytes=VMEM_LIMIT),
        name="adaln",
    )(c8, w_ada, b_ada)


N_LAT = Q_RANK + KV_RANK + ROPE
PREP_TN = 256
N_BIG_TILES = BIG_WIDTH // PREP_TN
N_LAT_TILES = LAT_WIDTH // PREP_TN
PREP_SHIFT = N_LAT % LANE


def _prep_kernel(main_ref, ext_ref, o_ref):
    j = pl.program_id(0)
    lane = lax.broadcasted_iota(jnp.int32, (D_MODEL, LANE), 1)

    @pl.when(j < N_BIG_TILES)
    def _():
        r0 = pltpu.roll(main_ref[:, :LANE], LANE - PREP_SHIFT, 1)
        r1 = pltpu.roll(main_ref[:, LANE:], LANE - PREP_SHIFT, 1)
        r2 = pltpu.roll(ext_ref[...], LANE - PREP_SHIFT, 1)
        o_ref[:, :LANE] = jnp.where(lane < LANE - PREP_SHIFT, r0, r1).astype(BF16)
        o_ref[:, LANE:] = jnp.where(lane < LANE - PREP_SHIFT, r1, r2).astype(BF16)

    @pl.when(j >= N_BIG_TILES)
    def _():
        col = (j - N_BIG_TILES) * PREP_TN + lax.broadcasted_iota(jnp.int32, (D_MODEL, PREP_TN), 1)
        o_ref[...] = jnp.where(col < N_LAT, main_ref[...], 0.0).astype(BF16)


def _prep_w_in(w_in):
    first = (N_LAT - PREP_SHIFT) // PREP_TN
    main = lambda j: (0, jnp.where(j < N_BIG_TILES, j + first, j - N_BIG_TILES))
    ext = lambda j: (0, jnp.where(j < N_BIG_TILES, (j + first + 1) * (PREP_TN // LANE), 0))
    return pl.pallas_call(
        _prep_kernel,
        grid=(N_BIG_TILES + N_LAT_TILES,),
        in_specs=[pl.BlockSpec((D_MODEL, PREP_TN), main), pl.BlockSpec((D_MODEL, LANE), ext)],
        out_specs=pl.BlockSpec((D_MODEL, PREP_TN), lambda j: (0, j)),
        out_shape=jax.ShapeDtypeStruct((D_MODEL, BIG_WIDTH + LAT_WIDTH), BF16),
        compiler_params=pltpu.CompilerParams(
            dimension_semantics=("parallel",), vmem_limit_bytes=VMEM_LIMIT),
        name="prep_w_in",
    )(w_in, w_in)


def _inproj_kernel(x_ref, g_ref, scale_ref, shift_ref, w_ref, o_ref, h_ref):
    @pl.when(pl.program_id(1) == 0)
    def _():
        y = _rms(x_ref[...], g_ref[...])
        h_ref[...] = (y * (1.0 + scale_ref[...]) + shift_ref[...]).astype(BF16)

    o_ref[...] = jnp.dot(h_ref[...], w_ref[...], preferred_element_type=F32).astype(o_ref.dtype)


def _inproj(x2, g, scale, shift, w_cat):
    tm, tn = 1024, 1536
    s, n = x2.shape[0], w_cat.shape[1]
    vec = pl.BlockSpec((1, D_MODEL), lambda i, j: (0, 0))
    return pl.pallas_call(
        _inproj_kernel,
        grid=(s // tm, n // tn),
        in_specs=[pl.BlockSpec((tm, D_MODEL), lambda i, j: (i, 0)), vec, vec, vec,
                  pl.BlockSpec((D_MODEL, tn), lambda i, j: (0, j))],
        out_specs=pl.BlockSpec((tm, tn), lambda i, j: (i, j)),
        out_shape=jax.ShapeDtypeStruct((s, n), BF16),
        scratch_shapes=[pltpu.VMEM((tm, D_MODEL), BF16)],
        compiler_params=pltpu.CompilerParams(
            dimension_semantics=("parallel", "arbitrary"), vmem_limit_bytes=VMEM_LIMIT),
        name="inproj",
    )(x2, g, scale, shift, w_cat)


def _rope_slab(t, cos_t, sin_a, sin_b):
    return t * cos_t + pltpu.roll(t, 96, 1) * sin_a + pltpu.roll(t, 32, 1) * sin_b


def _uproj_kernel(lat_ref, qg_ref, kvg_ref, wq_ref, wk_ref, wv_ref, cos_ref, sa_ref, sb_ref,
                  q_ref, k_ref, v_ref):
    lat = lat_ref[...].astype(F32)
    cos_t, sin_a, sin_b = cos_ref[...], sa_ref[...], sb_ref[...]

    qn = _rms(lat[:, :Q_RANK], qg_ref[...]).astype(BF16)
    q = jnp.dot(qn, wq_ref[...], preferred_element_type=F32)
    for h in range(HEADS):
        lo = h * HEAD_SLAB
        q_ref[:, lo:lo + LANE] = (q[:, lo:lo + LANE] * Q_SCALE).astype(BF16)
        rot = _rope_slab(q[:, lo + LANE:lo + HEAD_SLAB], cos_t, sin_a, sin_b)
        q_ref[:, lo + LANE:lo + HEAD_SLAB] = (rot * Q_SCALE).astype(BF16)

    kvn = _rms(lat[:, Q_RANK:Q_RANK + KV_RANK], kvg_ref[...]).astype(BF16)
    vv = jnp.dot(kvn, wv_ref[...], preferred_element_type=F32)
    kn = jnp.dot(kvn, wk_ref[...], preferred_element_type=F32)
    kr = _rope_slab(lat[:, Q_RANK + KV_RANK:Q_RANK + KV_RANK + LANE],
                    cos_t, sin_a, sin_b).astype(BF16)
    ones = jnp.ones((lat.shape[0], LANE), BF16)
    for h in range(HEADS):
        lo = h * HEAD_SLAB
        k_ref[:, lo:lo + LANE] = kn[:, h * LANE:(h + 1) * LANE].astype(BF16)
        k_ref[:, lo + LANE:lo + HEAD_SLAB] = kr
        v_ref[:, lo:lo + LANE] = vv[:, h * LANE:(h + 1) * LANE].astype(BF16)
        v_ref[:, lo + LANE:lo + HEAD_SLAB] = ones


def _uproj(proj, qg, kvg, wq, wk, wv, cos_t, sin_a, sin_b):
    tm = 512
    s = proj.shape[0]
    lat_block = BIG_WIDTH // LAT_WIDTH
    const = lambda i: (0, 0)
    row = lambda i: (i, 0)
    return pl.pallas_call(
        _uproj_kernel,
        grid=(s // tm,),
        in_specs=[pl.BlockSpec((tm, LAT_WIDTH), lambda i: (i, lat_block)),
                  pl.BlockSpec((1, Q_RANK), const), pl.BlockSpec((1, KV_RANK), const),
                  pl.BlockSpec(wq.shape, const), pl.BlockSpec(wk.shape, const),
                  pl.BlockSpec(wv.shape, const),
                  pl.BlockSpec((tm, LANE), row), pl.BlockSpec((tm, LANE), row),
                  pl.BlockSpec((tm, LANE), row)],
        out_specs=[pl.BlockSpec((tm, HEADS * HEAD_SLAB), row),
                   pl.BlockSpec((tm, HEADS * HEAD_SLAB), row),
                   pl.BlockSpec((tm, HEADS * HEAD_SLAB), row)],
        out_shape=[jax.ShapeDtypeStruct((s, HEADS * HEAD_SLAB), BF16)] * 3,
        compiler_params=pltpu.CompilerParams(
            dimension_semantics=("parallel",), vmem_limit_bytes=VMEM_LIMIT),
        name="uproj",
    )(proj, qg, kvg, wq, wk, wv, cos_t, sin_a, sin_b)


def _flash_kernel(q_ref, k_ref, v_ref, o_ref, s0_ref, s1_ref, m_ref, acc_ref, *, tq):
    i = pl.program_id(1)

    def scores(j, s_ref):
        start = pl.multiple_of(j * tq, tq)
        s_ref[...] = lax.dot_general(q_ref[...], k_ref[pl.ds(start, tq), :],
                                     (((1,), (1,)), ((), ())), preferred_element_type=F32)

    def update(j, s_ref, masked):
        start = pl.multiple_of(j * tq, tq)
        s = s_ref[...]
        if masked:
            rows = lax.broadcasted_iota(jnp.int32, (tq, tq), 0)
            cols = lax.broadcasted_iota(jnp.int32, (tq, tq), 1)
            s = jnp.where(cols <= rows, s, -jnp.inf)
        m_prev = m_ref[...]
        m_new = jnp.maximum(m_prev, jnp.max(s, axis=-1, keepdims=True))
        alpha = jnp.exp2(m_prev - m_new)
        p = jnp.exp2(s - jnp.concatenate([m_new] * (tq // LANE), axis=1))
        pv = jnp.dot(p.astype(BF16), v_ref[pl.ds(start, tq), :], preferred_element_type=F32)
        acc_ref[...] = jnp.concatenate([alpha, alpha], axis=1) * acc_ref[...] + pv
        m_ref[...] = m_new

    m_ref[...] = jnp.full(m_ref.shape, -jnp.inf, F32)
    acc_ref[...] = jnp.zeros(acc_ref.shape, F32)
    scores(0, s0_ref)

    def pair(t, carry):
        scores(2 * t + 1, s1_ref)
        update(2 * t, s0_ref, False)
        scores(2 * t + 2, s0_ref)
        update(2 * t + 1, s1_ref, False)
        return carry

    lax.fori_loop(0, i // 2, pair, 0)

    @pl.when(i % 2 == 0)
    def _():
        update(i, s0_ref, True)

    @pl.when(i % 2 == 1)
    def _():
        scores(i, s1_ref)
        update(i - 1, s0_ref, False)
        update(i, s1_ref, True)

    acc = acc_ref[...]
    o_ref[...] = (acc[:, :VDIM] / acc[:, VDIM:]).astype(o_ref.dtype)


def _flash(q, k, v1):
    tq = 512
    s = q.shape[0]
    return pl.pallas_call(
        functools.partial(_flash_kernel, tq=tq),
        grid=(HEADS, s // tq),
        in_specs=[pl.BlockSpec((tq, HEAD_SLAB), lambda h, i: (i, h)),
                  pl.BlockSpec((s, HEAD_SLAB), lambda h, i: (0, h)),
                  pl.BlockSpec((s, 2 * VDIM), lambda h, i: (0, h))],
        out_specs=pl.BlockSpec((tq, VDIM), lambda h, i: (i, h)),
        out_shape=jax.ShapeDtypeStruct((s, HEADS * VDIM), BF16),
        scratch_shapes=[pltpu.VMEM((tq, tq), F32), pltpu.VMEM((tq, tq), F32),
                        pltpu.VMEM((tq, LANE), F32), pltpu.VMEM((tq, 2 * VDIM), F32)],
        compiler_params=pltpu.CompilerParams(
            dimension_semantics=("parallel", "arbitrary"), vmem_limit_bytes=VMEM_LIMIT),
        name="flash",
    )(q, k, v1)


def _out_kernel(a_ref, zm_ref, u_ref, v_ref, zs_ref, gm_ref, gs_ref, x_ref, gate_ref,
                sg_ref, ws_ref, bs_ref, wo_ref, fg_ref, o_ref, mix_ref, *, tm):
    rows = lax.broadcasted_iota(jnp.int32, (CHUNK, CHUNK), 0)
    cols = lax.broadcasted_iota(jnp.int32, (CHUNK, CHUNK), 1)
    causal = cols <= rows

    vn = _rms(v_ref[...].astype(F32), sg_ref[...]).astype(BF16)
    for g in range(GROUPS):
        w = jnp.where(causal, ws_ref[g], 0.0).astype(BF16)
        for c in range(tm // CHUNK):
            r = slice(c * CHUNK, (c + 1) * CHUNK)
            col = slice(g * CHUNK, (g + 1) * CHUNK)
            mix_ref[r, col] = jnp.dot(w, vn[r, col], preferred_element_type=F32) + bs_ref[:, col]

    zm = zm_ref[...].astype(F32)
    zs = zs_ref[...].astype(F32)
    y_mla = a_ref[...].astype(F32) * (zm * _sigmoid(zm))
    y_sgu = u_ref[...].astype(F32) * mix_ref[...] * (zs * _sigmoid(zs))
    merged = (_sigmoid(gm_ref[...].astype(F32)) * y_mla
              + _sigmoid(gs_ref[...].astype(F32)) * y_sgu)
    y = jnp.dot(merged.astype(BF16), wo_ref[...], preferred_element_type=F32)
    o_ref[...] = _rms(x_ref[...] + gate_ref[...] * y, fg_ref[...])


def _outproj(attn, proj, x2, gate, sg, ws, bs_full, wo, fg):
    tm = 256
    s = x2.shape[0]
    row = lambda i: (i, 0)
    const = lambda i: (0, 0)
    big = lambda k: pl.BlockSpec((tm, D_MODEL), lambda i, k=k: (i, k))
    vec = pl.BlockSpec((1, D_MODEL), const)
    return pl.pallas_call(
        functools.partial(_out_kernel, tm=tm),
        grid=(s // tm,),
        in_specs=[pl.BlockSpec((tm, D_MODEL), row),
                  big(0), big(1), big(2), big(3), big(4), big(5),
                  pl.BlockSpec((tm, D_MODEL), row), vec, vec,
                  pl.BlockSpec(ws.shape, lambda i: (0, 0, 0)),
                  pl.BlockSpec(bs_full.shape, const),
                  pl.BlockSpec(wo.shape, const), vec],
        out_specs=pl.BlockSpec((tm, D_MODEL), row),
        out_shape=jax.ShapeDtypeStruct((s, D_MODEL), F32),
        scratch_shapes=[pltpu.VMEM((tm, D_MODEL), F32)],
        compiler_params=pltpu.CompilerParams(
            dimension_semantics=("parallel",), vmem_limit_bytes=VMEM_LIMIT),
        name="outproj",
    )(attn, proj, proj, proj, proj, proj, proj, x2, gate, sg, ws, bs_full, wo, fg)


def _rope_tables(positions):
    inv_freq = 1.0 / (ROPE_THETA ** (jnp.arange(0, ROPE, 2, dtype=F32) / ROPE))
    ang = positions.astype(F32)[:, None] * inv_freq
    cos, sin = jnp.cos(ang), jnp.sin(ang)
    z = jnp.zeros_like(cos)
    cos_t = jnp.concatenate([cos, cos, z, z], axis=-1)
    sin_a = jnp.concatenate([-sin, z, z, z], axis=-1)
    sin_b = jnp.concatenate([z, sin, z, z], axis=-1)
    return cos_t, sin_a, sin_b


def _layer(x2, c8, positions, attn_norm_g, w_ada, b_ada, w_in, q_norm_g, w_uq, kv_norm_g,
           w_ukv, sgu_norm_g, w_spatial, b_spatial, w_out):
    d = D_MODEL
    mod = _adaln(c8, w_ada, b_ada.reshape(1, 3 * d))[0:1]
    shift, scale, gate = mod[:, :d], mod[:, d:2 * d], mod[:, 2 * d:]

    proj = _inproj(x2, attn_norm_g.reshape(1, d), scale, shift, _prep_w_in(w_in))

    wq = w_uq.reshape(Q_RANK, HEADS, NOPE + ROPE)
    wq = jnp.pad(wq, ((0, 0), (0, 0), (0, HEAD_SLAB - NOPE - ROPE)))
    wq = wq.reshape(Q_RANK, HEADS * HEAD_SLAB).astype(BF16)
    wkv = w_ukv.reshape(KV_RANK, HEADS, NOPE + VDIM)
    wk = wkv[:, :, :NOPE].reshape(KV_RANK, HEADS * NOPE).astype(BF16)
    wv = wkv[:, :, NOPE:].reshape(KV_RANK, HEADS * VDIM).astype(BF16)
    cos_t, sin_a, sin_b = _rope_tables(positions)
    q, k, v = _uproj(proj, q_norm_g.reshape(1, Q_RANK), kv_norm_g.reshape(1, KV_RANK),
                     wq, wk, wv, cos_t, sin_a, sin_b)

    attn = _flash(q, k, v)

    bs_full = jnp.repeat(b_spatial.T, CHUNK, axis=1)
    return attn, proj, gate, bs_full


def kernel(x, c, positions, attn_norm_g, w_ada, b_ada, w_in, q_norm_g, w_uq, kv_norm_g, w_ukv,
           sgu_norm_g, w_spatial, b_spatial, w_out, final_norm_g):
    b, s, d = x.shape
    assert (b, s, d) == (1, SEQ, D_MODEL) and attn_norm_g.shape[0] == 1
    x2 = x.reshape(s, d)
    c8 = jnp.broadcast_to(c, (8, d))
    attn, proj, gate, bs_full = _layer(
        x2, c8, positions[0], attn_norm_g[0], w_ada[0], b_ada[0], w_in[0], q_norm_g[0],
        w_uq[0], kv_norm_g[0], w_ukv[0], sgu_norm_g[0], w_spatial[0], b_spatial[0], w_out[0])
    out = _outproj(attn, proj, x2, gate, sgu_norm_g[0].reshape(1, d), w_spatial[0], bs_full,
                   w_out[0].astype(BF16), final_norm_g.reshape(1, d))
    return out.reshape(b, s, d)
```

```python
import functools
import math

import jax
import jax.numpy as jnp
from jax import lax
from jax.experimental import pallas as pl
from jax.experimental.pallas import tpu as pltpu

D_MODEL = 2048
SEQ = 8192
HEADS = 16
NOPE = 128
ROPE = 64
VDIM = 128
Q_RANK = 768
KV_RANK = 512
ROPE_THETA = 10000.0
GROUPS = 16
CHUNK = 128
EPS = 1e-6

LANE = 128
HEAD_SLAB = 2 * LANE
LAT_WIDTH = 1536
BIG_WIDTH = 6 * D_MODEL
VMEM_LIMIT = 56 * 1024 * 1024

Q_SCALE = (1.0 / math.sqrt(NOPE + ROPE)) * math.log2(math.e)

BF16 = jnp.bfloat16
F32 = jnp.float32


def _sigmoid(x):
    return 0.5 * jnp.tanh(0.5 * x) + 0.5


def _rms(x, g):
    ms = jnp.mean(x * x, axis=-1, keepdims=True)
    return x * lax.rsqrt(ms + EPS) * g


def _adaln_kernel(c_ref, w_ref, b_ref, o_ref):
    c = c_ref[...]
    c_act = c * _sigmoid(c)
    o_ref[...] = jnp.dot(c_act.astype(BF16), w_ref[...].astype(BF16),
                         preferred_element_type=F32) + b_ref[...]


def _adaln(c8, w_ada, b_ada):
    tn = 1536
    n = w_ada.shape[1]
    return pl.pallas_call(
        _adaln_kernel,
        grid=(n // tn,),
        in_specs=[pl.BlockSpec((8, D_MODEL), lambda j: (0, 0)),
                  pl.BlockSpec((D_MODEL, tn), lambda j: (0, j)),
                  pl.BlockSpec((1, tn), lambda j: (0, j))],
        out_specs=pl.BlockSpec((8, tn), lambda j: (0, j)),
        out_shape=jax.ShapeDtypeStruct((8, n), F32),
        compiler_params=pltpu.CompilerParams(
            dimension_semantics=("arbitrary",), vmem_limit_bytes=VMEM_LIMIT),
        name="adaln",
    )(c8, w_ada, b_ada)


def _hnorm_kernel(x_ref, g_ref, scale_ref, shift_ref, h_ref):
    y = _rms(x_ref[...], g_ref[...])
    h_ref[...] = (y * (1.0 + scale_ref[...]) + shift_ref[...]).astype(BF16)


def _hnorm(x2, g, scale, shift):
    tm = 512
    s = x2.shape[0]
    vec = pl.BlockSpec((1, D_MODEL), lambda i: (0, 0))
    return pl.pallas_call(
        _hnorm_kernel,
        grid=(s // tm,),
        in_specs=[pl.BlockSpec((tm, D_MODEL), lambda i: (i, 0)), vec, vec, vec],
        out_specs=pl.BlockSpec((tm, D_MODEL), lambda i: (i, 0)),
        out_shape=jax.ShapeDtypeStruct((s, D_MODEL), BF16),
        compiler_params=pltpu.CompilerParams(
            dimension_semantics=("parallel",), vmem_limit_bytes=VMEM_LIMIT),
        name="hnorm",
    )(x2, g, scale, shift)


N_LAT = Q_RANK + KV_RANK + ROPE
INPROJ_TN = 1536
N_BIG_TILES = BIG_WIDTH // INPROJ_TN


def _inproj_kernel(h_ref, wt_ref, o_ref, wb_ref):
    j = pl.program_id(0)

    @pl.when(pl.program_id(1) == 0)
    def _():
        limit = jnp.where(j == N_BIG_TILES, N_LAT, INPROJ_TN)
        rows = lax.broadcasted_iota(jnp.int32, wt_ref.shape, 0)
        wb_ref[...] = jnp.where(rows < limit, wt_ref[...], 0.0).astype(BF16)

    o_ref[...] = lax.dot_general(h_ref[...], wb_ref[...], (((1,), (1,)), ((), ())),
                                 preferred_element_type=F32).astype(o_ref.dtype)


def _inproj(h, w_t):
    tm, tn = 1024, INPROJ_TN
    s = h.shape[0]
    sub = 8
    w_spec = pl.BlockSpec(
        (pl.Element(tn), pl.Element(D_MODEL)),
        lambda j, i: (jnp.where(j < N_BIG_TILES, N_LAT // sub + j * (tn // sub), 0) * sub, 0))
    return pl.pallas_call(
        _inproj_kernel,
        grid=(N_BIG_TILES + 1, s // tm),
        in_specs=[pl.BlockSpec((tm, D_MODEL), lambda j, i: (i, 0)), w_spec],
        out_specs=pl.BlockSpec((tm, tn), lambda j, i: (i, j)),
        out_shape=jax.ShapeDtypeStruct((s, BIG_WIDTH + LAT_WIDTH), BF16),
        scratch_shapes=[pltpu.VMEM((tn, D_MODEL), BF16)],
        compiler_params=pltpu.CompilerParams(
            dimension_semantics=("arbitrary", "arbitrary"), vmem_limit_bytes=VMEM_LIMIT),
        name="inproj",
    )(h, w_t)


def _rope_slab(t, cos_t, sin_a, sin_b):
    return t * cos_t + pltpu.roll(t, 96, 1) * sin_a + pltpu.roll(t, 32, 1) * sin_b


def _uproj_kernel(lat_ref, qg_ref, kvg_ref, wq_ref, wk_ref, wv_ref, cos_ref, sa_ref, sb_ref,
                  q_ref, k_ref, v_ref):
    lat = lat_ref[...].astype(F32)
    cos_t, sin_a, sin_b = cos_ref[...], sa_ref[...], sb_ref[...]

    qn = _rms(lat[:, :Q_RANK], qg_ref[...]).astype(BF16)
    q = jnp.dot(qn, wq_ref[...], preferred_element_type=F32)
    for h in range(HEADS):
        lo = h * HEAD_SLAB
        q_ref[:, lo:lo + LANE] = (q[:, lo:lo + LANE] * Q_SCALE).astype(BF16)
        rot = _rope_slab(q[:, lo + LANE:lo + HEAD_SLAB], cos_t, sin_a, sin_b)
        q_ref[:, lo + LANE:lo + HEAD_SLAB] = (rot * Q_SCALE).astype(BF16)

    kvn = _rms(lat[:, Q_RANK:Q_RANK + KV_RANK], kvg_ref[...]).astype(BF16)
    vv = jnp.dot(kvn, wv_ref[...], preferred_element_type=F32)
    kn = jnp.dot(kvn, wk_ref[...], preferred_element_type=F32)
    kr = _rope_slab(lat[:, Q_RANK + KV_RANK:Q_RANK + KV_RANK + LANE],
                    cos_t, sin_a, sin_b).astype(BF16)
    ones = jnp.ones((lat.shape[0], LANE), BF16)
    for h in range(HEADS):
        lo = h * HEAD_SLAB
        k_ref[:, lo:lo + LANE] = kn[:, h * LANE:(h + 1) * LANE].astype(BF16)
        k_ref[:, lo + LANE:lo + HEAD_SLAB] = kr
        v_ref[:, lo:lo + LANE] = vv[:, h * LANE:(h + 1) * LANE].astype(BF16)
        v_ref[:, lo + LANE:lo + HEAD_SLAB] = ones


def _uproj(proj, qg, kvg, wq, wk, wv, cos_t, sin_a, sin_b):
    tm = 512
    s = proj.shape[0]
    lat_block = BIG_WIDTH // LAT_WIDTH
    const = lambda i: (0, 0)
    row = lambda i: (i, 0)
    return pl.pallas_call(
        _uproj_kernel,
        grid=(s // tm,),
        in_specs=[pl.BlockSpec((tm, LAT_WIDTH), lambda i: (i, lat_block)),
                  pl.BlockSpec((1, Q_RANK), const), pl.BlockSpec((1, KV_RANK), const),
                  pl.BlockSpec(wq.shape, const), pl.BlockSpec(wk.shape, const),
                  pl.BlockSpec(wv.shape, const),
                  pl.BlockSpec((tm, LANE), row), pl.BlockSpec((tm, LANE), row),
                  pl.BlockSpec((tm, LANE), row)],
        out_specs=[pl.BlockSpec((tm, HEADS * HEAD_SLAB), row),
                   pl.BlockSpec((tm, HEADS * HEAD_SLAB), row),
                   pl.BlockSpec((tm, HEADS * HEAD_SLAB), row)],
        out_shape=[jax.ShapeDtypeStruct((s, HEADS * HEAD_SLAB), BF16)] * 3,
        compiler_params=pltpu.CompilerParams(
            dimension_semantics=("parallel",), vmem_limit_bytes=VMEM_LIMIT),
        name="uproj",
    )(proj, qg, kvg, wq, wk, wv, cos_t, sin_a, sin_b)


def _flash_kernel(q_ref, k_ref, v_ref, o_ref, s0_ref, s1_ref, m_ref, acc_ref, *, tq, tk):
    i = pl.program_id(1)

    def scores(j, s_ref, r0=0):
        start = pl.multiple_of(j * tk, tk)
        s_ref[r0:, :] = lax.dot_general(q_ref[r0:, :], k_ref[pl.ds(start, tk), :],
                                        (((1,), (1,)), ((), ())), preferred_element_type=F32)

    def update(j, s_ref, r0=0, col0=None):
        start = pl.multiple_of(j * tk, tk)
        s = s_ref[r0:, :]
        if col0 is not None:
            rows = r0 + lax.broadcasted_iota(jnp.int32, s.shape, 0)
            cols = col0 + lax.broadcasted_iota(jnp.int32, s.shape, 1)
            s = jnp.where(cols <= rows, s, -jnp.inf)
        m_prev = m_ref[r0:, :]
        m_new = jnp.maximum(m_prev, jnp.max(s, axis=-1, keepdims=True))
        alpha = jnp.exp2(m_prev - m_new)
        p = jnp.exp2(s - jnp.concatenate([m_new] * (tk // LANE), axis=1))
        pv = jnp.dot(p.astype(BF16), v_ref[pl.ds(start, tk), :], preferred_element_type=F32)
        acc_ref[r0:, :] = jnp.concatenate([alpha, alpha], axis=1) * acc_ref[r0:, :] + pv
        m_ref[r0:, :] = m_new

    m_ref[...] = jnp.full(m_ref.shape, -jnp.inf, F32)
    acc_ref[...] = jnp.zeros(acc_ref.shape, F32)
    scores(0, s0_ref)

    def pair(t, carry):
        scores(2 * t + 1, s1_ref)
        update(2 * t, s0_ref)
        scores(2 * t + 2, s0_ref)
        update(2 * t + 1, s1_ref)
        return carry

    lax.fori_loop(0, i, pair, 0)

    scores(2 * i + 1, s1_ref, r0=tk)
    update(2 * i, s0_ref, col0=0)
    update(2 * i + 1, s1_ref, r0=tk, col0=tk)

    acc = acc_ref[...]
    o_ref[...] = (acc[:, :VDIM] / acc[:, VDIM:]).astype(o_ref.dtype)


def _flash(q, k, v1):
    tq, tk = 1024, 512
    s = q.shape[0]
    return pl.pallas_call(
        functools.partial(_flash_kernel, tq=tq, tk=tk),
        grid=(HEADS, s // tq),
        in_specs=[pl.BlockSpec((tq, HEAD_SLAB), lambda h, i: (i, h)),
                  pl.BlockSpec((s, HEAD_SLAB), lambda h, i: (0, h)),
                  pl.BlockSpec((s, 2 * VDIM), lambda h, i: (0, h))],
        out_specs=pl.BlockSpec((tq, VDIM), lambda h, i: (i, h)),
        out_shape=jax.ShapeDtypeStruct((s, HEADS * VDIM), BF16),
        scratch_shapes=[pltpu.VMEM((tq, tk), F32), pltpu.VMEM((tq, tk), F32),
                        pltpu.VMEM((tq, LANE), F32), pltpu.VMEM((tq, 2 * VDIM), F32)],
        compiler_params=pltpu.CompilerParams(
            dimension_semantics=("parallel", "arbitrary"), vmem_limit_bytes=VMEM_LIMIT),
        name="flash",
    )(q, k, v1)


def _out_kernel(a_ref, zm_ref, u_ref, v_ref, zs_ref, gm_ref, gs_ref, x_ref, gate_ref,
                sg_ref, ws_ref, bs_ref, wo_ref, fg_ref, o_ref, mix_ref, *, tm):
    rows = lax.broadcasted_iota(jnp.int32, (CHUNK, CHUNK), 0)
    cols = lax.broadcasted_iota(jnp.int32, (CHUNK, CHUNK), 1)
    causal = cols <= rows

    vn = _rms(v_ref[...].astype(F32), sg_ref[...]).astype(BF16)
    for g in range(GROUPS):
        w = jnp.where(causal, ws_ref[g], 0.0).astype(BF16)
        for c in range(tm // CHUNK):
            r = slice(c * CHUNK, (c + 1) * CHUNK)
            col = slice(g * CHUNK, (g + 1) * CHUNK)
            mix_ref[r, col] = jnp.dot(w, vn[r, col], preferred_element_type=F32) + bs_ref[:, col]

    zm = zm_ref[...].astype(F32)
    zs = zs_ref[...].astype(F32)
    y_mla = a_ref[...].astype(F32) * (zm * _sigmoid(zm))
    y_sgu = u_ref[...].astype(F32) * mix_ref[...] * (zs * _sigmoid(zs))
    merged = (_sigmoid(gm_ref[...].astype(F32)) * y_mla
              + _sigmoid(gs_ref[...].astype(F32)) * y_sgu)
    y = jnp.dot(merged.astype(BF16), wo_ref[...], preferred_element_type=F32)
    o_ref[...] = _rms(x_ref[...] + gate_ref[...] * y, fg_ref[...])


def _outproj(attn, proj, x2, gate, sg, ws, bs_full, wo, fg):
    tm = 256
    s = x2.shape[0]
    row = lambda i: (i, 0)
    const = lambda i: (0, 0)
    big = lambda k: pl.BlockSpec((tm, D_MODEL), lambda i, k=k: (i, k))
    vec = pl.BlockSpec((1, D_MODEL), const)
    return pl.pallas_call(
        functools.partial(_out_kernel, tm=tm),
        grid=(s // tm,),
        in_specs=[pl.BlockSpec((tm, D_MODEL), row),
                  big(0), big(1), big(2), big(3), big(4), big(5),
                  pl.BlockSpec((tm, D_MODEL), row), vec, vec,
                  pl.BlockSpec(ws.shape, lambda i: (0, 0, 0)),
                  pl.BlockSpec(bs_full.shape, const),
                  pl.BlockSpec(wo.shape, const), vec],
        out_specs=pl.BlockSpec((tm, D_MODEL), row),
        out_shape=jax.ShapeDtypeStruct((s, D_MODEL), F32),
        scratch_shapes=[pltpu.VMEM((tm, D_MODEL), F32)],
        compiler_params=pltpu.CompilerParams(
            dimension_semantics=("parallel",), vmem_limit_bytes=VMEM_LIMIT),
        name="outproj",
    )(attn, proj, proj, proj, proj, proj, proj, x2, gate, sg, ws, bs_full, wo, fg)


def _rope_tables(positions):
    inv_freq = 1.0 / (ROPE_THETA ** (jnp.arange(0, ROPE, 2, dtype=F32) / ROPE))
    ang = positions.astype(F32)[:, None] * inv_freq
    cos, sin = jnp.cos(ang), jnp.sin(ang)
    z = jnp.zeros_like(cos)
    cos_t = jnp.concatenate([cos, cos, z, z], axis=-1)
    sin_a = jnp.concatenate([-sin, z, z, z], axis=-1)
    sin_b = jnp.concatenate([z, sin, z, z], axis=-1)
    return cos_t, sin_a, sin_b


def _layer(x2, c8, positions, attn_norm_g, w_ada, b_ada, w_in, q_norm_g, w_uq, kv_norm_g,
           w_ukv, sgu_norm_g, w_spatial, b_spatial, w_out):
    d = D_MODEL
    mod = _adaln(c8, w_ada, b_ada.reshape(1, 3 * d))[0:1]
    shift, scale, gate = mod[:, :d], mod[:, d:2 * d], mod[:, 2 * d:]

    h = _hnorm(x2, attn_norm_g.reshape(1, d), scale, shift)
    proj = _inproj(h, jnp.swapaxes(w_in, 0, 1))

    wq = w_uq.reshape(Q_RANK, HEADS, NOPE + ROPE)
    wq = jnp.pad(wq, ((0, 0), (0, 0), (0, HEAD_SLAB - NOPE - ROPE)))
    wq = wq.reshape(Q_RANK, HEADS * HEAD_SLAB).astype(BF16)
    wkv = w_ukv.reshape(KV_RANK, HEADS, NOPE + VDIM)
    wk = wkv[:, :, :NOPE].reshape(KV_RANK, HEADS * NOPE).astype(BF16)
    wv = wkv[:, :, NOPE:].reshape(KV_RANK, HEADS * VDIM).astype(BF16)
    cos_t, sin_a, sin_b = _rope_tables(positions)
    q, k, v = _uproj(proj, q_norm_g.reshape(1, Q_RANK), kv_norm_g.reshape(1, KV_RANK),
                     wq, wk, wv, cos_t, sin_a, sin_b)

    attn = _flash(q, k, v)

    bs_full = jnp.repeat(b_spatial.T, CHUNK, axis=1)
    return attn, proj, gate, bs_full


def kernel(x, c, positions, attn_norm_g, w_ada, b_ada, w_in, q_norm_g, w_uq, kv_norm_g, w_ukv,
           sgu_norm_g, w_spatial, b_spatial, w_out, final_norm_g):
    b, s, d = x.shape
    assert (b, s, d) == (1, SEQ, D_MODEL) and attn_norm_g.shape[0] == 1
    x2 = x.reshape(s, d)
    c8 = jnp.broadcast_to(c, (8, d))
    attn, proj, gate, bs_full = _layer(
        x2, c8, positions[0], attn_norm_g[0], w_ada[0], b_ada[0], w_in[0], q_norm_g[0],
        w_uq[0], kv_norm_g[0], w_ukv[0], sgu_norm_g[0], w_spatial[0], b_spatial[0], w_out[0])
    out = _outproj(attn, proj, x2, gate, sgu_norm_g[0].reshape(1, d), w_spatial[0], bs_full,
                   w_out[0].astype(BF16), final_norm_g.reshape(1, d))
    return out.reshape(b, s, d)
```

```python
import functools
import math

import jax
import jax.numpy as jnp
from jax import lax
from jax.experimental import pallas as pl
from jax.experimental.pallas import tpu as pltpu

D_MODEL = 2048
SEQ = 8192
HEADS = 16
NOPE = 128
ROPE = 64
VDIM = 128
Q_RANK = 768
KV_RANK = 512
ROPE_THETA = 10000.0
GROUPS = 16
CHUNK = 128
EPS = 1e-6

LANE = 128
HEAD_SLAB = 2 * LANE
LAT_WIDTH = 1536
BIG_WIDTH = 6 * D_MODEL
VMEM_LIMIT = 56 * 1024 * 1024

Q_SCALE = (1.0 / math.sqrt(NOPE + ROPE)) * math.log2(math.e)

BF16 = jnp.bfloat16
F32 = jnp.float32


def _sigmoid(x):
    return 0.5 * jnp.tanh(0.5 * x) + 0.5


def _rms(x, g):
    ms = jnp.mean(x * x, axis=-1, keepdims=True)
    return x * lax.rsqrt(ms + EPS) * g


def _adaln_kernel(c_ref, w_ref, b_ref, o_ref):
    c = c_ref[...]
    c_act = c * _sigmoid(c)
    o_ref[...] = jnp.dot(c_act.astype(BF16), w_ref[...].astype(BF16),
                         preferred_element_type=F32) + b_ref[...]


def _adaln(c8, w_ada, b_ada):
    tn = 1536
    n = w_ada.shape[1]
    return pl.pallas_call(
        _adaln_kernel,
        grid=(n // tn,),
        in_specs=[pl.BlockSpec((8, D_MODEL), lambda j: (0, 0)),
                  pl.BlockSpec((D_MODEL, tn), lambda j: (0, j)),
                  pl.BlockSpec((1, tn), lambda j: (0, j))],
        out_specs=pl.BlockSpec((8, tn), lambda j: (0, j)),
        out_shape=jax.ShapeDtypeStruct((8, n), F32),
        compiler_params=pltpu.CompilerParams(
            dimension_semantics=("arbitrary",), vmem_limit_bytes=VMEM_LIMIT),
        name="adaln",
    )(c8, w_ada, b_ada)


def _hnorm_kernel(x_ref, g_ref, scale_ref, shift_ref, h_ref):
    y = _rms(x_ref[...], g_ref[...])
    h_ref[...] = (y * (1.0 + scale_ref[...]) + shift_ref[...]).astype(BF16)


def _hnorm(x2, g, scale, shift):
    tm = 512
    s = x2.shape[0]
    vec = pl.BlockSpec((1, D_MODEL), lambda i: (0, 0))
    return pl.pallas_call(
        _hnorm_kernel,
        grid=(s // tm,),
        in_specs=[pl.BlockSpec((tm, D_MODEL), lambda i: (i, 0)), vec, vec, vec],
        out_specs=pl.BlockSpec((tm, D_MODEL), lambda i: (i, 0)),
        out_shape=jax.ShapeDtypeStruct((s, D_MODEL), BF16),
        compiler_params=pltpu.CompilerParams(
            dimension_semantics=("parallel",), vmem_limit_bytes=VMEM_LIMIT),
        name="hnorm",
    )(x2, g, scale, shift)


N_LAT = Q_RANK + KV_RANK + ROPE
INPROJ_TN = 1536
N_BIG_TILES = BIG_WIDTH // INPROJ_TN


def _inproj_kernel(h_ref, wt_ref, o_ref, wb_ref):
    j = pl.program_id(0)

    @pl.when(pl.program_id(1) == 0)
    def _():
        rows = lax.broadcasted_iota(jnp.int32, (INPROJ_TN, 1), 0)
        feat = j * INPROJ_TN + rows
        is_lat = j == N_BIG_TILES
        halved = jnp.logical_not(is_lat) & ((feat < D_MODEL) | (feat >= 3 * D_MODEL))
        keep = jnp.logical_not(is_lat) | (rows < N_LAT)
        w = wt_ref[...] * jnp.where(halved, 0.5, 1.0)
        wb_ref[...] = jnp.where(keep, w, 0.0).astype(BF16)

    o_ref[...] = lax.dot_general(h_ref[...], wb_ref[...], (((1,), (1,)), ((), ())),
                                 preferred_element_type=F32).astype(o_ref.dtype)


def _inproj(h, w_t):
    tm, tn = 1024, INPROJ_TN
    s = h.shape[0]
    sub = 8
    w_spec = pl.BlockSpec(
        (pl.Element(tn), pl.Element(D_MODEL)),
        lambda j, i: (jnp.where(j < N_BIG_TILES, N_LAT // sub + j * (tn // sub), 0) * sub, 0))
    return pl.pallas_call(
        _inproj_kernel,
        grid=(N_BIG_TILES + 1, s // tm),
        in_specs=[pl.BlockSpec((tm, D_MODEL), lambda j, i: (i, 0)), w_spec],
        out_specs=pl.BlockSpec((tm, tn), lambda j, i: (i, j)),
        out_shape=jax.ShapeDtypeStruct((s, BIG_WIDTH + LAT_WIDTH), BF16),
        scratch_shapes=[pltpu.VMEM((tn, D_MODEL), BF16)],
        compiler_params=pltpu.CompilerParams(
            dimension_semantics=("arbitrary", "arbitrary"), vmem_limit_bytes=VMEM_LIMIT),
        name="inproj",
    )(h, w_t)


def _rope_slab(t, cos_t, sin_a, sin_b):
    return t * cos_t + pltpu.roll(t, 96, 1) * sin_a + pltpu.roll(t, 32, 1) * sin_b


def _uproj_kernel(lat_ref, qg_ref, kvg_ref, wq_ref, wk_ref, wv_ref, cos_ref, sa_ref, sb_ref,
                  q_ref, k_ref, v_ref):
    lat = lat_ref[...].astype(F32)
    cos_t, sin_a, sin_b = cos_ref[...], sa_ref[...], sb_ref[...]

    qn = _rms(lat[:, :Q_RANK], qg_ref[...]).astype(BF16)
    q = jnp.dot(qn, wq_ref[...], preferred_element_type=F32)
    for h in range(HEADS):
        lo = h * HEAD_SLAB
        q_ref[:, lo:lo + LANE] = (q[:, lo:lo + LANE] * Q_SCALE).astype(BF16)
        rot = _rope_slab(q[:, lo + LANE:lo + HEAD_SLAB], cos_t, sin_a, sin_b)
        q_ref[:, lo + LANE:lo + HEAD_SLAB] = (rot * Q_SCALE).astype(BF16)

    kvn = _rms(lat[:, Q_RANK:Q_RANK + KV_RANK], kvg_ref[...]).astype(BF16)
    vv = jnp.dot(kvn, wv_ref[...], preferred_element_type=F32)
    kn = jnp.dot(kvn, wk_ref[...], preferred_element_type=F32)
    kr = _rope_slab(lat[:, Q_RANK + KV_RANK:Q_RANK + KV_RANK + LANE],
                    cos_t, sin_a, sin_b).astype(BF16)
    ones = jnp.ones((lat.shape[0], LANE), BF16)
    for h in range(HEADS):
        lo = h * HEAD_SLAB
        k_ref[:, lo:lo + LANE] = kn[:, h * LANE:(h + 1) * LANE].astype(BF16)
        k_ref[:, lo + LANE:lo + HEAD_SLAB] = kr
        v_ref[:, lo:lo + LANE] = vv[:, h * LANE:(h + 1) * LANE].astype(BF16)
        v_ref[:, lo + LANE:lo + HEAD_SLAB] = ones


def _uproj(proj, qg, kvg, wq, wk, wv, cos_t, sin_a, sin_b):
    tm = 512
    s = proj.shape[0]
    lat_block = BIG_WIDTH // LAT_WIDTH
    const = lambda i: (0, 0)
    row = lambda i: (i, 0)
    return pl.pallas_call(
        _uproj_kernel,
        grid=(s // tm,),
        in_specs=[pl.BlockSpec((tm, LAT_WIDTH), lambda i: (i, lat_block)),
                  pl.BlockSpec((1, Q_RANK), const), pl.BlockSpec((1, KV_RANK), const),
                  pl.BlockSpec(wq.shape, const), pl.BlockSpec(wk.shape, const),
                  pl.BlockSpec(wv.shape, const),
                  pl.BlockSpec((tm, LANE), row), pl.BlockSpec((tm, LANE), row),
                  pl.BlockSpec((tm, LANE), row)],
        out_specs=[pl.BlockSpec((tm, HEADS * HEAD_SLAB), row),
                   pl.BlockSpec((tm, HEADS * HEAD_SLAB), row),
                   pl.BlockSpec((tm, HEADS * HEAD_SLAB), row)],
        out_shape=[jax.ShapeDtypeStruct((s, HEADS * HEAD_SLAB), BF16)] * 3,
        compiler_params=pltpu.CompilerParams(
            dimension_semantics=("parallel",), vmem_limit_bytes=VMEM_LIMIT),
        name="uproj",
    )(proj, qg, kvg, wq, wk, wv, cos_t, sin_a, sin_b)


def _flash_kernel(q_ref, k_ref, v_ref, o_ref, s0_ref, s1_ref, m_ref, acc_ref, *, tq, tk):
    i = pl.program_id(1)
    n_diag = tq // tk
    bufs = (s0_ref, s1_ref)

    assert n_diag % 2 == 0

    def scores_into(s_ref, j, r0=0):
        start = pl.multiple_of(j * tk, tk)
        s_ref[r0:, :] = lax.dot_general(q_ref[r0:, :], k_ref[pl.ds(start, tk), :],
                                        (((1,), (1,)), ((), ())), preferred_element_type=F32)

    def update(s_ref, j, r0=0, causal=False):
        start = pl.multiple_of(j * tk, tk)
        s = s_ref[r0:, :]
        if causal:
            rows = lax.broadcasted_iota(jnp.int32, (tk, tk), 0)
            cols = lax.broadcasted_iota(jnp.int32, (tk, tk), 1)
            top = jnp.where(cols <= rows, s[:tk], -jnp.inf)
            s = top if s.shape[0] == tk else jnp.concatenate([top, s[tk:]], axis=0)
        m_prev = m_ref[r0:, :]
        m_new = jnp.maximum(m_prev, jnp.max(s, axis=-1, keepdims=True))
        alpha = jnp.exp2(m_prev - m_new)
        p = jnp.exp2(s - jnp.concatenate([m_new] * (tk // LANE), axis=1))
        pv = jnp.dot(p.astype(BF16), v_ref[pl.ds(start, tk), :], preferred_element_type=F32)
        acc_ref[r0:, :] = jnp.concatenate([alpha, alpha], axis=1) * acc_ref[r0:, :] + pv
        m_ref[r0:, :] = m_new

    m_ref[...] = jnp.full(m_ref.shape, -jnp.inf, F32)
    acc_ref[...] = jnp.zeros(acc_ref.shape, F32)
    scores_into(s0_ref, 0)

    def full_blocks(t, carry):
        for u in range(n_diag):
            scores_into(bufs[(u + 1) % 2], n_diag * t + u + 1)
            update(bufs[u % 2], n_diag * t + u)
        return carry

    lax.fori_loop(0, i, full_blocks, 0)

    for u in range(n_diag):
        if u + 1 < n_diag:
            scores_into(bufs[(u + 1) % 2], n_diag * i + u + 1, r0=(u + 1) * tk)
        update(bufs[u % 2], n_diag * i + u, r0=u * tk, causal=True)

    acc = acc_ref[...]
    o_ref[...] = (acc[:, :VDIM] / acc[:, VDIM:]).astype(o_ref.dtype)


def _flash(q, k, v1):
    tq, tk = 2048, 512
    s = q.shape[0]
    return pl.pallas_call(
        functools.partial(_flash_kernel, tq=tq, tk=tk),
        grid=(HEADS, s // tq),
        in_specs=[pl.BlockSpec((tq, HEAD_SLAB), lambda h, i: (i, h)),
                  pl.BlockSpec((s, HEAD_SLAB), lambda h, i: (0, h)),
                  pl.BlockSpec((s, 2 * VDIM), lambda h, i: (0, h))],
        out_specs=pl.BlockSpec((tq, VDIM), lambda h, i: (i, h)),
        out_shape=jax.ShapeDtypeStruct((s, HEADS * VDIM), BF16),
        scratch_shapes=[pltpu.VMEM((tq, tk), F32), pltpu.VMEM((tq, tk), F32),
                        pltpu.VMEM((tq, LANE), F32), pltpu.VMEM((tq, 2 * VDIM), F32)],
        compiler_params=pltpu.CompilerParams(
            dimension_semantics=("parallel", "arbitrary"), vmem_limit_bytes=VMEM_LIMIT),
        name="flash",
    )(q, k, v1)


def _out_kernel(a_ref, zm_ref, u_ref, v_ref, zs_ref, gm_ref, gs_ref, x_ref, gate_ref,
                sg_ref, ws_ref, bs_ref, wo_ref, fg_ref, o_ref, mix_ref, *, tm):
    rows = lax.broadcasted_iota(jnp.int32, (CHUNK, CHUNK), 0)
    cols = lax.broadcasted_iota(jnp.int32, (CHUNK, CHUNK), 1)
    causal = cols <= rows

    vn = _rms(v_ref[...].astype(F32), sg_ref[...]).astype(BF16)
    for g in range(GROUPS):
        w = jnp.where(causal, ws_ref[g], 0.0).astype(BF16)
        for c in range(tm // CHUNK):
            r = slice(c * CHUNK, (c + 1) * CHUNK)
            col = slice(g * CHUNK, (g + 1) * CHUNK)
            mix_ref[r, col] = jnp.dot(w, vn[r, col], preferred_element_type=F32) + bs_ref[:, col]

    zm = zm_ref[...].astype(F32)
    zs = zs_ref[...].astype(F32)
    y_mla = (jnp.tanh(gm_ref[...].astype(F32)) + 1.0) * (jnp.tanh(zm) + 1.0) * zm \
        * a_ref[...].astype(F32)
    y_sgu = (jnp.tanh(gs_ref[...].astype(F32)) + 1.0) * (jnp.tanh(zs) + 1.0) * zs \
        * u_ref[...].astype(F32) * mix_ref[...]
    twice_merged = (y_mla + y_sgu).astype(BF16)
    y = jnp.dot(twice_merged, wo_ref[...], preferred_element_type=F32)
    o_ref[...] = _rms(x_ref[...] + (0.5 * gate_ref[...]) * y, fg_ref[...])


def _outproj(attn, proj, x2, gate, sg, ws, bs_full, wo, fg):
    tm = 256
    s = x2.shape[0]
    row = lambda i: (i, 0)
    const = lambda i: (0, 0)
    big = lambda k: pl.BlockSpec((tm, D_MODEL), lambda i, k=k: (i, k))
    vec = pl.BlockSpec((1, D_MODEL), const)
    return pl.pallas_call(
        functools.partial(_out_kernel, tm=tm),
        grid=(s // tm,),
        in_specs=[pl.BlockSpec((tm, D_MODEL), row),
                  big(0), big(1), big(2), big(3), big(4), big(5),
                  pl.BlockSpec((tm, D_MODEL), row), vec, vec,
                  pl.BlockSpec(ws.shape, lambda i: (0, 0, 0)),
                  pl.BlockSpec(bs_full.shape, const),
                  pl.BlockSpec(wo.shape, const), vec],
        out_specs=pl.BlockSpec((tm, D_MODEL), row),
        out_shape=jax.ShapeDtypeStruct((s, D_MODEL), F32),
        scratch_shapes=[pltpu.VMEM((tm, D_MODEL), F32)],
        compiler_params=pltpu.CompilerParams(
            dimension_semantics=("parallel",), vmem_limit_bytes=VMEM_LIMIT),
        name="outproj",
    )(attn, proj, proj, proj, proj, proj, proj, x2, gate, sg, ws, bs_full, wo, fg)


def _rope_tables(positions):
    inv_freq = 1.0 / (ROPE_THETA ** (jnp.arange(0, ROPE, 2, dtype=F32) / ROPE))
    ang = positions.astype(F32)[:, None] * inv_freq
    cos, sin = jnp.cos(ang), jnp.sin(ang)
    z = jnp.zeros_like(cos)
    cos_t = jnp.concatenate([cos, cos, z, z], axis=-1)
    sin_a = jnp.concatenate([-sin, z, z, z], axis=-1)
    sin_b = jnp.concatenate([z, sin, z, z], axis=-1)
    return cos_t, sin_a, sin_b


def _layer(x2, c8, positions, attn_norm_g, w_ada, b_ada, w_in, q_norm_g, w_uq, kv_norm_g,
           w_ukv, sgu_norm_g, w_spatial, b_spatial, w_out):
    d = D_MODEL
    mod = _adaln(c8, w_ada, b_ada.reshape(1, 3 * d))[0:1]
    shift, scale, gate = mod[:, :d], mod[:, d:2 * d], mod[:, 2 * d:]

    h = _hnorm(x2, attn_norm_g.reshape(1, d), scale, shift)
    proj = _inproj(h, jnp.swapaxes(w_in, 0, 1))

    wq = w_uq.reshape(Q_RANK, HEADS, NOPE + ROPE)
    wq = jnp.pad(wq, ((0, 0), (0, 0), (0, HEAD_SLAB - NOPE - ROPE)))
    wq = wq.reshape(Q_RANK, HEADS * HEAD_SLAB).astype(BF16)
    wkv = w_ukv.reshape(KV_RANK, HEADS, NOPE + VDIM)
    wk = wkv[:, :, :NOPE].reshape(KV_RANK, HEADS * NOPE).astype(BF16)
    wv = wkv[:, :, NOPE:].reshape(KV_RANK, HEADS * VDIM).astype(BF16)
    cos_t, sin_a, sin_b = _rope_tables(positions)
    q, k, v = _uproj(proj, q_norm_g.reshape(1, Q_RANK), kv_norm_g.reshape(1, KV_RANK),
                     wq, wk, wv, cos_t, sin_a, sin_b)

    attn = _flash(q, k, v)

    bs_full = jnp.repeat(b_spatial.T, CHUNK, axis=1)
    return attn, proj, gate, bs_full


def kernel(x, c, positions, attn_norm_g, w_ada, b_ada, w_in, q_norm_g, w_uq, kv_norm_g, w_ukv,
           sgu_norm_g, w_spatial, b_spatial, w_out, final_norm_g):
    b, s, d = x.shape
    assert (b, s, d) == (1, SEQ, D_MODEL) and attn_norm_g.shape[0] == 1
    x2 = x.reshape(s, d)
    c8 = jnp.broadcast_to(c, (8, d))
    attn, proj, gate, bs_full = _layer(
        x2, c8, positions[0], attn_norm_g[0], w_ada[0], b_ada[0], w_in[0], q_norm_g[0],
        w_uq[0], kv_norm_g[0], w_ukv[0], sgu_norm_g[0], w_spatial[0], b_spatial[0], w_out[0])
    out = _outproj(attn, proj, x2, gate, sgu_norm_g[0].reshape(1, d), w_spatial[0], bs_full,
                   w_out[0].astype(BF16), final_norm_g.reshape(1, d))
    return out.reshape(b, s, d)
```

```python
import functools
import math

import jax
import jax.numpy as jnp
from jax import lax
from jax.experimental import pallas as pl
from jax.experimental.pallas import tpu as pltpu

D_MODEL = 2048
SEQ = 8192
HEADS = 16
NOPE = 128
ROPE = 64
VDIM = 128
Q_RANK = 768
KV_RANK = 512
ROPE_THETA = 10000.0
GROUPS = 16
CHUNK = 128
EPS = 1e-6

LANE = 128
HEAD_SLAB = 2 * LANE
N_LAT = Q_RANK + KV_RANK + ROPE
LAT_WIDTH = N_LAT + ROPE
BIG_WIDTH = 6 * D_MODEL
VMEM_LIMIT = 56 * 1024 * 1024

Q_SCALE = (1.0 / math.sqrt(NOPE + ROPE)) * math.log2(math.e)

BF16 = jnp.bfloat16
F32 = jnp.float32
NT_DIMS = (((1,), (1,)), ((), ()))


def _sigmoid(x):
    return 0.5 * jnp.tanh(0.5 * x) + 0.5


def _rms(x, g):
    ms = jnp.mean(x * x, axis=-1, keepdims=True)
    return x * lax.rsqrt(ms + EPS) * g


def _resident(shape):
    return pl.BlockSpec(shape, lambda *_: (0,) * len(shape), pipeline_mode=pl.Buffered(1))


def _adaln_kernel(c_ref, w_ref, b_ref, o_ref):
    c = c_ref[...]
    c_act = c * _sigmoid(c)
    o_ref[...] = jnp.dot(c_act.astype(BF16), w_ref[...].astype(BF16),
                         preferred_element_type=F32) + b_ref[...]


def _adaln(c8, w_ada, b_ada):
    tn = 1536
    n = w_ada.shape[1]
    return pl.pallas_call(
        _adaln_kernel,
        grid=(n // tn,),
        in_specs=[pl.BlockSpec((8, D_MODEL), lambda j: (0, 0)),
                  pl.BlockSpec((D_MODEL, tn), lambda j: (0, j)),
                  pl.BlockSpec((1, tn), lambda j: (0, j))],
        out_specs=pl.BlockSpec((8, tn), lambda j: (0, j)),
        out_shape=jax.ShapeDtypeStruct((8, n), F32),
        compiler_params=pltpu.CompilerParams(
            dimension_semantics=("arbitrary",), vmem_limit_bytes=VMEM_LIMIT),
        name="adaln",
    )(c8, w_ada, b_ada)


def _rope(t, cos4, sin_a, sin_b):
    return t * cos4 + pltpu.roll(t, 96, 1) * sin_a + pltpu.roll(t, 32, 1) * sin_b


def _front_kernel(x_ref, g_ref, scale_ref, shift_ref, wlat_ref, qg_ref, kvg_ref, wqn_ref,
                  wqr_ref, wk_ref, wv_ref, cos_ref, sa_ref, sb_ref, h_ref, q_ref, k_ref, v_ref):
    y = _rms(x_ref[...], g_ref[...])
    h = (y * (1.0 + scale_ref[...]) + shift_ref[...]).astype(BF16)
    h_ref[...] = h
    lat = lax.dot_general(h, wlat_ref[...], NT_DIMS, preferred_element_type=F32)

    cos4, sin_a, sin_b = cos_ref[...], sa_ref[...], sb_ref[...]
    low_half = lax.broadcasted_iota(jnp.int32, cos4.shape, 1) < ROPE

    qn = _rms(lat[:, :Q_RANK], qg_ref[...]).astype(BF16)
    q_nope = jnp.dot(qn, wqn_ref[...], preferred_element_type=F32)
    q_rope = jnp.dot(qn, wqr_ref[...], preferred_element_type=F32)
    for pair in range(HEADS // 2):
        rot = _rope(q_rope[:, pair * LANE:(pair + 1) * LANE], cos4, sin_a, sin_b) * Q_SCALE
        for odd, slab in enumerate((rot, pltpu.roll(rot, ROPE, 1))):
            lo = (2 * pair + odd) * HEAD_SLAB
            q_ref[:, lo + LANE:lo + HEAD_SLAB] = jnp.where(low_half, slab, 0.0).astype(BF16)
    for hd in range(HEADS):
        lo = hd * HEAD_SLAB
        q_ref[:, lo:lo + LANE] = (q_nope[:, hd * LANE:(hd + 1) * LANE] * Q_SCALE).astype(BF16)

    kvn = _rms(lat[:, Q_RANK:Q_RANK + KV_RANK], kvg_ref[...]).astype(BF16)
    vv = jnp.dot(kvn, wv_ref[...], preferred_element_type=F32)
    kn = jnp.dot(kvn, wk_ref[...], preferred_element_type=F32)
    kr = _rope(lat[:, Q_RANK + KV_RANK:Q_RANK + KV_RANK + LANE], cos4, sin_a, sin_b)
    kr = jnp.where(low_half, kr, 0.0).astype(BF16)
    ones = jnp.ones(kr.shape, BF16)
    for hd in range(HEADS):
        lo = hd * HEAD_SLAB
        k_ref[:, lo:lo + LANE] = kn[:, hd * LANE:(hd + 1) * LANE].astype(BF16)
        k_ref[:, lo + LANE:lo + HEAD_SLAB] = kr
        v_ref[:, lo:lo + LANE] = vv[:, hd * LANE:(hd + 1) * LANE].astype(BF16)
        v_ref[:, lo + LANE:lo + HEAD_SLAB] = ones


def _front(x2, g, scale, shift, wlat, qg, kvg, wqn, wqr, wk, wv, cos4, sin_a, sin_b):
    tm = 256
    s = x2.shape[0]
    row = lambda i: (i, 0)
    tab = pl.BlockSpec((tm, LANE), row)
    slab = pl.BlockSpec((tm, HEADS * HEAD_SLAB), row)
    return pl.pallas_call(
        _front_kernel,
        grid=(s // tm,),
        in_specs=[pl.BlockSpec((tm, D_MODEL), row),
                  _resident(g.shape), _resident(scale.shape), _resident(shift.shape),
                  _resident(wlat.shape), _resident(qg.shape), _resident(kvg.shape),
                  _resident(wqn.shape), _resident(wqr.shape), _resident(wk.shape),
                  _resident(wv.shape), tab, tab, tab],
        out_specs=[pl.BlockSpec((tm, D_MODEL), row), slab, slab, slab],
        out_shape=[jax.ShapeDtypeStruct((s, D_MODEL), BF16)]
        + [jax.ShapeDtypeStruct((s, HEADS * HEAD_SLAB), BF16)] * 3,
        compiler_params=pltpu.CompilerParams(
            dimension_semantics=("parallel",), vmem_limit_bytes=VMEM_LIMIT),
        name="front",
    )(x2, g, scale, shift, wlat, qg, kvg, wqn, wqr, wk, wv, cos4, sin_a, sin_b)


INPROJ_TN = 1536
N_BIG_TILES = BIG_WIDTH // INPROJ_TN


def _inproj_kernel(h_ref, wt_ref, o_ref, wb_ref):
    @pl.when(pl.program_id(1) == 0)
    def _():
        rows = lax.broadcasted_iota(jnp.int32, (INPROJ_TN, 1), 0)
        feat = pl.program_id(0) * INPROJ_TN + rows
        halved = (feat < D_MODEL) | (feat >= 3 * D_MODEL)
        wb_ref[...] = (wt_ref[...] * jnp.where(halved, 0.5, 1.0)).astype(BF16)

    o_ref[...] = lax.dot_general(h_ref[...], wb_ref[...], NT_DIMS,
                                 preferred_element_type=F32).astype(o_ref.dtype)


def _inproj(h, w_t):
    tm, tn = 1024, INPROJ_TN
    s = h.shape[0]
    sub = 8
    w_spec = pl.BlockSpec((pl.Element(tn), pl.Element(D_MODEL)),
                          lambda j, i: ((N_LAT // sub + j * (tn // sub)) * sub, 0))
    return pl.pallas_call(
        _inproj_kernel,
        grid=(N_BIG_TILES, s // tm),
        in_specs=[pl.BlockSpec((tm, D_MODEL), lambda j, i: (i, 0)), w_spec],
        out_specs=pl.BlockSpec((tm, tn), lambda j, i: (i, j)),
        out_shape=jax.ShapeDtypeStruct((s, BIG_WIDTH), BF16),
        scratch_shapes=[pltpu.VMEM((tn, D_MODEL), BF16)],
        compiler_params=pltpu.CompilerParams(
            dimension_semantics=("arbitrary", "arbitrary"), vmem_limit_bytes=VMEM_LIMIT),
        name="inproj",
    )(h, w_t)


def _flash_kernel(q_ref, k_ref, v_ref, o_ref, s0_ref, s1_ref, m_ref, acc_ref, *, tq, tk):
    i = pl.program_id(1)
    n_diag = tq // tk
    bufs = (s0_ref, s1_ref)
    assert n_diag % 2 == 0

    def scores_into(s_ref, j, r0=0):
        start = pl.multiple_of(j * tk, tk)
        s_ref[r0:, :] = lax.dot_general(q_ref[r0:, :], k_ref[pl.ds(start, tk), :], NT_DIMS,
                                        preferred_element_type=F32)

    def update(s_ref, j, r0=0, causal=False):
        start = pl.multiple_of(j * tk, tk)
        s = s_ref[r0:, :]
        if causal:
            rows = lax.broadcasted_iota(jnp.int32, (tk, tk), 0)
            cols = lax.broadcasted_iota(jnp.int32, (tk, tk), 1)
            top = jnp.where(cols <= rows, s[:tk], -jnp.inf)
            s = top if s.shape[0] == tk else jnp.concatenate([top, s[tk:]], axis=0)
        m_prev = m_ref[r0:, :]
        m_new = jnp.maximum(m_prev, jnp.max(s, axis=-1, keepdims=True))
        alpha = jnp.exp2(m_prev - m_new)
        p = jnp.exp2(s - jnp.concatenate([m_new] * (tk // LANE), axis=1))
        pv = jnp.dot(p.astype(BF16), v_ref[pl.ds(start, tk), :], preferred_element_type=F32)
        acc_ref[r0:, :] = jnp.concatenate([alpha, alpha], axis=1) * acc_ref[r0:, :] + pv
        m_ref[r0:, :] = m_new

    m_ref[...] = jnp.full(m_ref.shape, -jnp.inf, F32)
    acc_ref[...] = jnp.zeros(acc_ref.shape, F32)
    scores_into(s0_ref, 0)

    def full_blocks(t, carry):
        for u in range(n_diag):
            scores_into(bufs[(u + 1) % 2], n_diag * t + u + 1)
            update(bufs[u % 2], n_diag * t + u)
        return carry

    lax.fori_loop(0, i, full_blocks, 0)

    for u in range(n_diag):
        if u + 1 < n_diag:
            scores_into(bufs[(u + 1) % 2], n_diag * i + u + 1, r0=(u + 1) * tk)
        update(bufs[u % 2], n_diag * i + u, r0=u * tk, causal=True)

    acc = acc_ref[...]
    o_ref[...] = (acc[:, :VDIM] / acc[:, VDIM:]).astype(o_ref.dtype)


def _flash(q, k, v1):
    tq, tk = 2048, 512
    s = q.shape[0]
    return pl.pallas_call(
        functools.partial(_flash_kernel, tq=tq, tk=tk),
        grid=(HEADS, s // tq),
        in_specs=[pl.BlockSpec((tq, HEAD_SLAB), lambda h, i: (i, h)),
                  pl.BlockSpec((s, HEAD_SLAB), lambda h, i: (0, h)),
                  pl.BlockSpec((s, 2 * VDIM), lambda h, i: (0, h))],
        out_specs=pl.BlockSpec((tq, VDIM), lambda h, i: (i, h)),
        out_shape=jax.ShapeDtypeStruct((s, HEADS * VDIM), BF16),
        scratch_shapes=[pltpu.VMEM((tq, tk), F32), pltpu.VMEM((tq, tk), F32),
                        pltpu.VMEM((tq, LANE), F32), pltpu.VMEM((tq, 2 * VDIM), F32)],
        compiler_params=pltpu.CompilerParams(
            dimension_semantics=("parallel", "arbitrary"), vmem_limit_bytes=VMEM_LIMIT),
        name="flash",
    )(q, k, v1)


def _out_kernel(a_ref, zm_ref, u_ref, v_ref, zs_ref, gm_ref, gs_ref, x_ref, gate_ref,
                sg_ref, ws_ref, bs_ref, wo_ref, fg_ref, o_ref, mix_ref, *, tm):
    rows = lax.broadcasted_iota(jnp.int32, (CHUNK, CHUNK), 0)
    cols = lax.broadcasted_iota(jnp.int32, (CHUNK, CHUNK), 1)
    causal = cols <= rows

    vn = _rms(v_ref[...].astype(F32), sg_ref[...]).astype(BF16)
    for g in range(GROUPS):
        w = jnp.where(causal, ws_ref[g], 0.0).astype(BF16)
        for c in range(tm // CHUNK):
            r = slice(c * CHUNK, (c + 1) * CHUNK)
            col = slice(g * CHUNK, (g + 1) * CHUNK)
            mix_ref[r, col] = jnp.dot(w, vn[r, col], preferred_element_type=F32) + bs_ref[:, col]

    def gating(g_ref, z_ref):
        z = z_ref[...]
        return ((jnp.tanh(g_ref[...]) + 1.0) * (jnp.tanh(z) + 1.0) * z).astype(F32)

    y_mla = gating(gm_ref, zm_ref) * a_ref[...].astype(F32)
    y_sgu = gating(gs_ref, zs_ref) * u_ref[...].astype(F32) * mix_ref[...]
    twice_merged = (y_mla + y_sgu).astype(BF16)
    y = jnp.dot(twice_merged, wo_ref[...], preferred_element_type=F32)
    o_ref[...] = _rms(x_ref[...] + (0.5 * gate_ref[...]) * y, fg_ref[...])


def _outproj(attn, proj, x2, gate, sg, ws, bs_full, wo, fg):
    tm = 256
    s = x2.shape[0]
    row = lambda i: (i, 0)
    big = lambda k: pl.BlockSpec((tm, D_MODEL), lambda i, k=k: (i, k))
    return pl.pallas_call(
        functools.partial(_out_kernel, tm=tm),
        grid=(s // tm,),
        in_specs=[pl.BlockSpec((tm, D_MODEL), row),
                  big(0), big(1), big(2), big(3), big(4), big(5),
                  pl.BlockSpec((tm, D_MODEL), row), _resident(gate.shape), _resident(sg.shape),
                  _resident(ws.shape), _resident(bs_full.shape), _resident(wo.shape),
                  _resident(fg.shape)],
        out_specs=pl.BlockSpec((tm, D_MODEL), row),
        out_shape=jax.ShapeDtypeStruct((s, D_MODEL), F32),
        scratch_shapes=[pltpu.VMEM((tm, D_MODEL), F32)],
        compiler_params=pltpu.CompilerParams(
            dimension_semantics=("parallel",), vmem_limit_bytes=VMEM_LIMIT),
        name="outproj",
    )(attn, proj, proj, proj, proj, proj, proj, x2, gate, sg, ws, bs_full, wo, fg)


def _rope_tables(positions):
    inv_freq = 1.0 / (ROPE_THETA ** (jnp.arange(0, ROPE, 2, dtype=F32) / ROPE))
    ang = positions.astype(F32)[:, None] * inv_freq
    cos, sin = jnp.cos(ang), jnp.sin(ang)
    z = jnp.zeros_like(cos)
    cos4 = jnp.concatenate([cos, cos, cos, cos], axis=-1)
    sin_a = jnp.concatenate([-sin, z, -sin, z], axis=-1)
    sin_b = jnp.concatenate([z, sin, z, sin], axis=-1)
    return cos4, sin_a, sin_b


def kernel(x, c, positions, attn_norm_g, w_ada, b_ada, w_in, q_norm_g, w_uq, kv_norm_g, w_ukv,
           sgu_norm_g, w_spatial, b_spatial, w_out, final_norm_g):
    b, s, d = x.shape
    assert (b, s, d) == (1, SEQ, D_MODEL) and attn_norm_g.shape[0] == 1
    x2 = x.reshape(s, d)

    mod = _adaln(jnp.broadcast_to(c, (8, d)), w_ada[0], b_ada[0].reshape(1, 3 * d))[0:1]
    shift, scale, gate = mod[:, :d], mod[:, d:2 * d], mod[:, 2 * d:]

    w_t = jnp.swapaxes(w_in[0], 0, 1)
    wlat = w_t[:LAT_WIDTH].astype(BF16)
    wq = w_uq[0].reshape(Q_RANK, HEADS, NOPE + ROPE)
    wqn = wq[:, :, :NOPE].reshape(Q_RANK, HEADS * NOPE).astype(BF16)
    wqr = wq[:, :, NOPE:].reshape(Q_RANK, HEADS * ROPE).astype(BF16)
    wkv = w_ukv[0].reshape(KV_RANK, HEADS, NOPE + VDIM)
    wk = wkv[:, :, :NOPE].reshape(KV_RANK, HEADS * NOPE).astype(BF16)
    wv = wkv[:, :, NOPE:].reshape(KV_RANK, HEADS * VDIM).astype(BF16)
    h, q, k, v1 = _front(x2, attn_norm_g[0].reshape(1, d), scale, shift, wlat,
                         q_norm_g[0].reshape(1, Q_RANK), kv_norm_g[0].reshape(1, KV_RANK),
                         wqn, wqr, wk, wv, *_rope_tables(positions[0]))

    proj = _inproj(h, w_t)
    attn = _flash(q, k, v1)

    bs_full = jnp.repeat(b_spatial[0].T, CHUNK, axis=1)
    out = _outproj(attn, proj, x2, gate, sgu_norm_g[0].reshape(1, d), w_spatial[0], bs_full,
                   w_out[0].astype(BF16), final_norm_g.reshape(1, d))
    return out.reshape(b, s, d)
```

```python
import functools
import math

import jax
import jax.numpy as jnp
from jax import lax
from jax.experimental import pallas as pl
from jax.experimental.pallas import tpu as pltpu

D_MODEL = 2048
SEQ = 8192
HEADS = 16
NOPE = 128
ROPE = 64
VDIM = 128
Q_RANK = 768
KV_RANK = 512
ROPE_THETA = 10000.0
GROUPS = 16
CHUNK = 128
EPS = 1e-6

LANE = 128
HEAD_SLAB = 2 * LANE
KV_BLOCK = 512
N_LAT = Q_RANK + KV_RANK + ROPE
LAT_WIDTH = N_LAT + ROPE
BIG_WIDTH = 6 * D_MODEL
VMEM_LIMIT = 56 * 1024 * 1024

Q_SCALE = (1.0 / math.sqrt(NOPE + ROPE)) * math.log2(math.e)

BF16 = jnp.bfloat16
F32 = jnp.float32
NT_DIMS = (((1,), (1,)), ((), ()))


def _sigmoid(x):
    return 0.5 * jnp.tanh(0.5 * x) + 0.5


def _rms(x, g):
    ms = jnp.mean(x * x, axis=-1, keepdims=True)
    return x * lax.rsqrt(ms + EPS) * g


def _resident(shape):
    return pl.BlockSpec(shape, lambda *_: (0,) * len(shape), pipeline_mode=pl.Buffered(1))


def _adaln_kernel(c_ref, w_ref, b_ref, o_ref):
    c = c_ref[...]
    c_act = c * _sigmoid(c)
    o_ref[...] = jnp.dot(c_act.astype(BF16), w_ref[...].astype(BF16),
                         preferred_element_type=F32) + b_ref[...]


def _adaln(c8, w_ada, b_ada):
    tn = 1536
    n = w_ada.shape[1]
    return pl.pallas_call(
        _adaln_kernel,
        grid=(n // tn,),
        in_specs=[pl.BlockSpec((8, D_MODEL), lambda j: (0, 0)),
                  pl.BlockSpec((D_MODEL, tn), lambda j: (0, j)),
                  pl.BlockSpec((1, tn), lambda j: (0, j))],
        out_specs=pl.BlockSpec((8, tn), lambda j: (0, j)),
        out_shape=jax.ShapeDtypeStruct((8, n), F32),
        compiler_params=pltpu.CompilerParams(
            dimension_semantics=("arbitrary",), vmem_limit_bytes=VMEM_LIMIT),
        name="adaln",
    )(c8, w_ada, b_ada)


def _rope(t, cos4, sin_a, sin_b):
    return t * cos4 + pltpu.roll(t, 96, 1) * sin_a + pltpu.roll(t, 32, 1) * sin_b


def _front_kernel(x_ref, g_ref, scale_ref, shift_ref, wlat_ref, qg_ref, kvg_ref, wqn_ref,
                  wqr_ref, wk_ref, wvt_ref, cos_ref, sa_ref, sb_ref, h_ref, q_ref, k_ref, vt_ref):
    y = _rms(x_ref[...], g_ref[...])
    h = (y * (1.0 + scale_ref[...]) + shift_ref[...]).astype(BF16)
    h_ref[...] = h
    lat = lax.dot_general(h, wlat_ref[...], NT_DIMS, preferred_element_type=F32)

    cos4, sin_a, sin_b = cos_ref[...], sa_ref[...], sb_ref[...]
    low_half = lax.broadcasted_iota(jnp.int32, cos4.shape, 1) < ROPE

    qn = _rms(lat[:, :Q_RANK], qg_ref[...]).astype(BF16)
    q_nope = jnp.dot(qn, wqn_ref[...], preferred_element_type=F32)
    q_rope = jnp.dot(qn, wqr_ref[...], preferred_element_type=F32)
    for pair in range(HEADS // 2):
        rot = _rope(q_rope[:, pair * LANE:(pair + 1) * LANE], cos4, sin_a, sin_b) * Q_SCALE
        for odd, slab in enumerate((rot, pltpu.roll(rot, ROPE, 1))):
            lo = (2 * pair + odd) * HEAD_SLAB
            q_ref[:, lo + LANE:lo + HEAD_SLAB] = jnp.where(low_half, slab, 0.0).astype(BF16)
    for hd in range(HEADS):
        lo = hd * HEAD_SLAB
        q_ref[:, lo:lo + LANE] = (q_nope[:, hd * LANE:(hd + 1) * LANE] * Q_SCALE).astype(BF16)

    kvn = _rms(lat[:, Q_RANK:Q_RANK + KV_RANK], kvg_ref[...]).astype(BF16)
    vt = lax.dot_general(wvt_ref[...], kvn, NT_DIMS, preferred_element_type=F32)
    vt_ref[...] = vt.astype(BF16).reshape(vt_ref.shape)
    kn = jnp.dot(kvn, wk_ref[...], preferred_element_type=F32)
    kr = _rope(lat[:, Q_RANK + KV_RANK:Q_RANK + KV_RANK + LANE], cos4, sin_a, sin_b)
    kr = jnp.where(low_half, kr, 0.0).astype(BF16)
    for hd in range(HEADS):
        lo = hd * HEAD_SLAB
        k_ref[:, lo:lo + LANE] = kn[:, hd * LANE:(hd + 1) * LANE].astype(BF16)
        k_ref[:, lo + LANE:lo + HEAD_SLAB] = kr


def _front(x2, g, scale, shift, wlat, qg, kvg, wqn, wqr, wk, wvt, cos4, sin_a, sin_b):
    tm = 256
    s = x2.shape[0]
    per_block = KV_BLOCK // tm
    row = lambda i: (i, 0)
    tab = pl.BlockSpec((tm, LANE), row)
    slab = pl.BlockSpec((tm, HEADS * HEAD_SLAB), row)
    vt_spec = pl.BlockSpec((HEADS, 1, VDIM, tm), lambda i: (0, i // per_block, 0, i % per_block))
    return pl.pallas_call(
        _front_kernel,
        grid=(s // tm,),
        in_specs=[pl.BlockSpec((tm, D_MODEL), row),
                  _resident(g.shape), _resident(scale.shape), _resident(shift.shape),
                  _resident(wlat.shape), _resident(qg.shape), _resident(kvg.shape),
                  _resident(wqn.shape), _resident(wqr.shape), _resident(wk.shape),
                  _resident(wvt.shape), tab, tab, tab],
        out_specs=[pl.BlockSpec((tm, D_MODEL), row), slab, slab, vt_spec],
        out_shape=[jax.ShapeDtypeStruct((s, D_MODEL), BF16),
                   jax.ShapeDtypeStruct((s, HEADS * HEAD_SLAB), BF16),
                   jax.ShapeDtypeStruct((s, HEADS * HEAD_SLAB), BF16),
                   jax.ShapeDtypeStruct((HEADS, s // KV_BLOCK, VDIM, KV_BLOCK), BF16)],
        compiler_params=pltpu.CompilerParams(
            dimension_semantics=("parallel",), vmem_limit_bytes=VMEM_LIMIT),
        name="front",
    )(x2, g, scale, shift, wlat, qg, kvg, wqn, wqr, wk, wvt, cos4, sin_a, sin_b)


INPROJ_TN = 1536
N_BIG_TILES = BIG_WIDTH // INPROJ_TN


def _inproj_kernel(h_ref, wt_ref, o_ref, wb_ref):
    @pl.when(pl.program_id(1) == 0)
    def _():
        rows = lax.broadcasted_iota(jnp.int32, (INPROJ_TN, 1), 0)
        feat = pl.program_id(0) * INPROJ_TN + rows
        halved = (feat < D_MODEL) | (feat >= 3 * D_MODEL)
        wb_ref[...] = (wt_ref[...] * jnp.where(halved, 0.5, 1.0)).astype(BF16)

    o_ref[...] = lax.dot_general(h_ref[...], wb_ref[...], NT_DIMS,
                                 preferred_element_type=F32).astype(o_ref.dtype)


def _inproj(h, w_t):
    tm, tn = 1024, INPROJ_TN
    s = h.shape[0]
    sub = 8
    w_spec = pl.BlockSpec((pl.Element(tn), pl.Element(D_MODEL)),
                          lambda j, i: ((N_LAT // sub + j * (tn // sub)) * sub, 0))
    return pl.pallas_call(
        _inproj_kernel,
        grid=(N_BIG_TILES, s // tm),
        in_specs=[pl.BlockSpec((tm, D_MODEL), lambda j, i: (i, 0)), w_spec],
        out_specs=pl.BlockSpec((tm, tn), lambda j, i: (i, j)),
        out_shape=jax.ShapeDtypeStruct((s, BIG_WIDTH), BF16),
        scratch_shapes=[pltpu.VMEM((tn, D_MODEL), BF16)],
        compiler_params=pltpu.CompilerParams(
            dimension_semantics=("arbitrary", "arbitrary"), vmem_limit_bytes=VMEM_LIMIT),
        name="inproj",
    )(h, w_t)


ONES_ROWS = 16


def _flash_kernel(q_ref, k_ref, vt_ref, o_ref, s0_ref, s1_ref, m_ref, acc_ref, *, tq, tk):
    i = pl.program_id(1)
    n_diag = tq // tk
    bufs = (s0_ref, s1_ref)
    assert n_diag % 2 == 0

    def scores_into(s_ref, j, c0=0):
        start = pl.multiple_of(j * tk, tk)
        s_ref[:, c0:] = lax.dot_general(k_ref[pl.ds(start, tk), :], q_ref[c0:, :], NT_DIMS,
                                        preferred_element_type=F32)

    def update(s_ref, j, c0=0, causal=False):
        v_ext = jnp.concatenate([vt_ref[0, j], jnp.ones((ONES_ROWS, tk), BF16)], axis=0)
        for c in range(c0, tq, tk):
            s = s_ref[:, c:c + tk]
            if causal and c == c0:
                kv = lax.broadcasted_iota(jnp.int32, (tk, tk), 0)
                qq = lax.broadcasted_iota(jnp.int32, (tk, tk), 1)
                s = jnp.where(kv <= qq, s, -jnp.inf)
            m_prev = m_ref[:, c:c + tk]
            m_new = jnp.maximum(m_prev, jnp.max(s, axis=0, keepdims=True))
            alpha = jnp.exp2(m_prev - m_new)
            p = jnp.exp2(s - m_new).astype(BF16)
            pv = jnp.dot(v_ext, p, preferred_element_type=F32)
            acc_ref[:, c:c + tk] = alpha * acc_ref[:, c:c + tk] + pv
            m_ref[:, c:c + tk] = m_new

    m_ref[...] = jnp.full(m_ref.shape, -jnp.inf, F32)
    acc_ref[...] = jnp.zeros(acc_ref.shape, F32)
    scores_into(s0_ref, 0)

    def full_blocks(t, carry):
        for u in range(n_diag):
            scores_into(bufs[(u + 1) % 2], n_diag * t + u + 1)
            update(bufs[u % 2], n_diag * t + u)
        return carry

    lax.fori_loop(0, i, full_blocks, 0)

    for u in range(n_diag):
        if u + 1 < n_diag:
            scores_into(bufs[(u + 1) % 2], n_diag * i + u + 1, c0=(u + 1) * tk)
        update(bufs[u % 2], n_diag * i + u, c0=u * tk, causal=True)

    o_ref[...] = (acc_ref[:VDIM, :] / acc_ref[VDIM:VDIM + 1, :]).T.astype(o_ref.dtype)


def _flash(q, k, vt):
    tq, tk = 2048, KV_BLOCK
    s = q.shape[0]
    return pl.pallas_call(
        functools.partial(_flash_kernel, tq=tq, tk=tk),
        grid=(HEADS, s // tq),
        in_specs=[pl.BlockSpec((tq, HEAD_SLAB), lambda h, i: (i, h)),
                  pl.BlockSpec((s, HEAD_SLAB), lambda h, i: (0, h)),
                  pl.BlockSpec((1, s // tk, VDIM, tk), lambda h, i: (h, 0, 0, 0))],
        out_specs=pl.BlockSpec((tq, VDIM), lambda h, i: (i, h)),
        out_shape=jax.ShapeDtypeStruct((s, HEADS * VDIM), BF16),
        scratch_shapes=[pltpu.VMEM((tk, tq), F32), pltpu.VMEM((tk, tq), F32),
                        pltpu.VMEM((1, tq), F32), pltpu.VMEM((VDIM + ONES_ROWS, tq), F32)],
        compiler_params=pltpu.CompilerParams(
            dimension_semantics=("parallel", "arbitrary"), vmem_limit_bytes=VMEM_LIMIT),
        name="flash",
    )(q, k, vt)


def _out_kernel(a_ref, zm_ref, u_ref, v_ref, zs_ref, gm_ref, gs_ref, x_ref, gate_ref,
                sg_ref, ws_ref, bs_ref, wo_ref, fg_ref, o_ref, mix_ref, *, tm):
    rows = lax.broadcasted_iota(jnp.int32, (CHUNK, CHUNK), 0)
    cols = lax.broadcasted_iota(jnp.int32, (CHUNK, CHUNK), 1)
    causal = cols <= rows

    vn = _rms(v_ref[...].astype(F32), sg_ref[...]).astype(BF16)
    for g in range(GROUPS):
        w = jnp.where(causal, ws_ref[g], 0.0).astype(BF16)
        for c in range(tm // CHUNK):
            r = slice(c * CHUNK, (c + 1) * CHUNK)
            col = slice(g * CHUNK, (g + 1) * CHUNK)
            mix_ref[r, col] = jnp.dot(w, vn[r, col], preferred_element_type=F32) + bs_ref[:, col]

    def gating(g_ref, z_ref):
        z = z_ref[...]
        return ((jnp.tanh(g_ref[...]) + 1.0) * (jnp.tanh(z) + 1.0) * z).astype(F32)

    y_mla = gating(gm_ref, zm_ref) * a_ref[...].astype(F32)
    y_sgu = gating(gs_ref, zs_ref) * u_ref[...].astype(F32) * mix_ref[...]
    twice_merged = (y_mla + y_sgu).astype(BF16)
    y = jnp.dot(twice_merged, wo_ref[...], preferred_element_type=F32)
    o_ref[...] = _rms(x_ref[...] + (0.5 * gate_ref[...]) * y, fg_ref[...])


def _outproj(attn, proj, x2, gate, sg, ws, bs_full, wo, fg):
    tm = 256
    s = x2.shape[0]
    row = lambda i: (i, 0)
    big = lambda k: pl.BlockSpec((tm, D_MODEL), lambda i, k=k: (i, k))
    return pl.pallas_call(
        functools.partial(_out_kernel, tm=tm),
        grid=(s // tm,),
        in_specs=[pl.BlockSpec((tm, D_MODEL), row),
                  big(0), big(1), big(2), big(3), big(4), big(5),
                  pl.BlockSpec((tm, D_MODEL), row), _resident(gate.shape), _resident(sg.shape),
                  _resident(ws.shape), _resident(bs_full.shape), _resident(wo.shape),
                  _resident(fg.shape)],
        out_specs=pl.BlockSpec((tm, D_MODEL), row),
        out_shape=jax.ShapeDtypeStruct((s, D_MODEL), F32),
        scratch_shapes=[pltpu.VMEM((tm, D_MODEL), F32)],
        compiler_params=pltpu.CompilerParams(
            dimension_semantics=("parallel",), vmem_limit_bytes=VMEM_LIMIT),
        name="outproj",
    )(attn, proj, proj, proj, proj, proj, proj, x2, gate, sg, ws, bs_full, wo, fg)


def _rope_tables(positions):
    inv_freq = 1.0 / (ROPE_THETA ** (jnp.arange(0, ROPE, 2, dtype=F32) / ROPE))
    ang = positions.astype(F32)[:, None] * jnp.tile(inv_freq, LANE // (ROPE // 2))
    sin = jnp.sin(ang)
    first = (jnp.arange(LANE) % ROPE) < ROPE // 2
    return jnp.cos(ang), jnp.where(first, -sin, 0.0), jnp.where(first, 0.0, sin)


def kernel(x, c, positions, attn_norm_g, w_ada, b_ada, w_in, q_norm_g, w_uq, kv_norm_g, w_ukv,
           sgu_norm_g, w_spatial, b_spatial, w_out, final_norm_g):
    b, s, d = x.shape
    assert (b, s, d) == (1, SEQ, D_MODEL) and attn_norm_g.shape[0] == 1
    x2 = x.reshape(s, d)

    mod = _adaln(jnp.broadcast_to(c, (8, d)), w_ada[0], b_ada[0].reshape(1, 3 * d))[0:1]
    shift, scale, gate = mod[:, :d], mod[:, d:2 * d], mod[:, 2 * d:]

    w_t = jnp.swapaxes(w_in[0], 0, 1)
    wlat = w_t[:LAT_WIDTH].astype(BF16)
    wq = w_uq[0].reshape(Q_RANK, HEADS, NOPE + ROPE)
    wqn = wq[:, :, :NOPE].reshape(Q_RANK, HEADS * NOPE).astype(BF16)
    wqr = wq[:, :, NOPE:].reshape(Q_RANK, HEADS * ROPE).astype(BF16)
    wkv = w_ukv[0].reshape(KV_RANK, HEADS, NOPE + VDIM)
    wk = wkv[:, :, :NOPE].reshape(KV_RANK, HEADS * NOPE).astype(BF16)
    wvt = wkv[:, :, NOPE:].reshape(KV_RANK, HEADS * VDIM).T.astype(BF16)
    h, q, k, vt = _front(x2, attn_norm_g[0].reshape(1, d), scale, shift, wlat,
                         q_norm_g[0].reshape(1, Q_RANK), kv_norm_g[0].reshape(1, KV_RANK),
                         wqn, wqr, wk, wvt, *_rope_tables(positions[0]))

    proj = _inproj(h, w_t)
    attn = _flash(q, k, vt)

    bs_full = jnp.repeat(b_spatial[0].T, CHUNK, axis=1)
    out = _outproj(attn, proj, x2, gate, sgu_norm_g[0].reshape(1, d), w_spatial[0], bs_full,
                   w_out[0].astype(BF16), final_norm_g.reshape(1, d))
    return out.reshape(b, s, d)
```

```python
import functools
import math

import jax
import jax.numpy as jnp
from jax import lax
from jax.experimental import pallas as pl
from jax.experimental.pallas import tpu as pltpu

D_MODEL = 2048
SEQ = 8192
HEADS = 16
NOPE = 128
ROPE = 64
VDIM = 128
Q_RANK = 768
KV_RANK = 512
ROPE_THETA = 10000.0
GROUPS = 16
CHUNK = 128
EPS = 1e-6

LANE = 128
HEAD_SLAB = 2 * LANE
KV_BLOCK = 512
N_LAT = Q_RANK + KV_RANK + ROPE
LAT_WIDTH = N_LAT + ROPE
BIG_WIDTH = 6 * D_MODEL
VMEM_LIMIT = 56 * 1024 * 1024

Q_SCALE = (1.0 / math.sqrt(NOPE + ROPE)) * math.log2(math.e)

BF16 = jnp.bfloat16
F32 = jnp.float32
NT_DIMS = (((1,), (1,)), ((), ()))


def _sigmoid(x):
    return 0.5 * jnp.tanh(0.5 * x) + 0.5


def _rms(x, g):
    ms = jnp.mean(x * x, axis=-1, keepdims=True)
    return x * lax.rsqrt(ms + EPS) * g


def _resident(shape):
    return pl.BlockSpec(shape, lambda *_: (0,) * len(shape), pipeline_mode=pl.Buffered(1))


def _adaln_kernel(c_ref, w_ref, b_ref, o_ref):
    c = c_ref[...]
    c_act = c * _sigmoid(c)
    o_ref[...] = jnp.dot(c_act.astype(BF16), w_ref[...].astype(BF16),
                         preferred_element_type=F32) + b_ref[...]


def _adaln(c8, w_ada, b_ada):
    tn = 1536
    n = w_ada.shape[1]
    return pl.pallas_call(
        _adaln_kernel,
        grid=(n // tn,),
        in_specs=[pl.BlockSpec((8, D_MODEL), lambda j: (0, 0)),
                  pl.BlockSpec((D_MODEL, tn), lambda j: (0, j)),
                  pl.BlockSpec((1, tn), lambda j: (0, j))],
        out_specs=pl.BlockSpec((8, tn), lambda j: (0, j)),
        out_shape=jax.ShapeDtypeStruct((8, n), F32),
        compiler_params=pltpu.CompilerParams(
            dimension_semantics=("arbitrary",), vmem_limit_bytes=VMEM_LIMIT),
        name="adaln",
    )(c8, w_ada, b_ada)


def _rope(t, cos4, sin_a, sin_b):
    return t * cos4 + pltpu.roll(t, 96, 1) * sin_a + pltpu.roll(t, 32, 1) * sin_b


def _front_kernel(x_ref, g_ref, scale_ref, shift_ref, wlat_ref, qg_ref, kvg_ref, wqn_ref,
                  wqr_ref, wk_ref, wvt_ref, cos_ref, sa_ref, sb_ref, h_ref, q_ref, k_ref, vt_ref):
    y = _rms(x_ref[...], g_ref[...])
    h = (y * (1.0 + scale_ref[...]) + shift_ref[...]).astype(BF16)
    h_ref[...] = h
    lat = lax.dot_general(h, wlat_ref[...], NT_DIMS, preferred_element_type=F32)

    cos4, sin_a, sin_b = cos_ref[...], sa_ref[...], sb_ref[...]
    low_half = lax.broadcasted_iota(jnp.int32, cos4.shape, 1) < ROPE

    qn = _rms(lat[:, :Q_RANK], qg_ref[...]).astype(BF16)
    q_nope = jnp.dot(qn, wqn_ref[...], preferred_element_type=F32)
    q_rope = jnp.dot(qn, wqr_ref[...], preferred_element_type=F32)
    for pair in range(HEADS // 2):
        rot = _rope(q_rope[:, pair * LANE:(pair + 1) * LANE], cos4, sin_a, sin_b) * Q_SCALE
        for odd, slab in enumerate((rot, pltpu.roll(rot, ROPE, 1))):
            lo = (2 * pair + odd) * HEAD_SLAB
            q_ref[:, lo + LANE:lo + HEAD_SLAB] = jnp.where(low_half, slab, 0.0).astype(BF16)
    for hd in range(HEADS):
        lo = hd * HEAD_SLAB
        q_ref[:, lo:lo + LANE] = (q_nope[:, hd * LANE:(hd + 1) * LANE] * Q_SCALE).astype(BF16)

    kvn = _rms(lat[:, Q_RANK:Q_RANK + KV_RANK], kvg_ref[...]).astype(BF16)
    vt = lax.dot_general(wvt_ref[...], kvn, NT_DIMS, preferred_element_type=F32)
    vt_ref[...] = vt.astype(BF16).reshape(vt_ref.shape)
    kn = jnp.dot(kvn, wk_ref[...], preferred_element_type=F32)
    kr = _rope(lat[:, Q_RANK + KV_RANK:Q_RANK + KV_RANK + LANE], cos4, sin_a, sin_b)
    kr = jnp.where(low_half, kr, 0.0).astype(BF16)
    for hd in range(HEADS):
        lo = hd * HEAD_SLAB
        k_ref[:, lo:lo + LANE] = kn[:, hd * LANE:(hd + 1) * LANE].astype(BF16)
        k_ref[:, lo + LANE:lo + HEAD_SLAB] = kr


def _front(x2, g, scale, shift, wlat, qg, kvg, wqn, wqr, wk, wvt, cos4, sin_a, sin_b):
    tm = 256
    s = x2.shape[0]
    per_block = KV_BLOCK // tm
    row = lambda i: (i, 0)
    tab = pl.BlockSpec((tm, LANE), row)
    slab = pl.BlockSpec((tm, HEADS * HEAD_SLAB), row)
    vt_spec = pl.BlockSpec((HEADS, 1, VDIM, tm), lambda i: (0, i // per_block, 0, i % per_block))
    return pl.pallas_call(
        _front_kernel,
        grid=(s // tm,),
        in_specs=[pl.BlockSpec((tm, D_MODEL), row),
                  _resident(g.shape), _resident(scale.shape), _resident(shift.shape),
                  _resident(wlat.shape), _resident(qg.shape), _resident(kvg.shape),
                  _resident(wqn.shape), _resident(wqr.shape), _resident(wk.shape),
                  _resident(wvt.shape), tab, tab, tab],
        out_specs=[pl.BlockSpec((tm, D_MODEL), row), slab, slab, vt_spec],
        out_shape=[jax.ShapeDtypeStruct((s, D_MODEL), BF16),
                   jax.ShapeDtypeStruct((s, HEADS * HEAD_SLAB), BF16),
                   jax.ShapeDtypeStruct((s, HEADS * HEAD_SLAB), BF16),
                   jax.ShapeDtypeStruct((HEADS, s // KV_BLOCK, VDIM, KV_BLOCK), BF16)],
        compiler_params=pltpu.CompilerParams(
            dimension_semantics=("parallel",), vmem_limit_bytes=VMEM_LIMIT),
        name="front",
    )(x2, g, scale, shift, wlat, qg, kvg, wqn, wqr, wk, wvt, cos4, sin_a, sin_b)


INPROJ_TN = 1536
N_BIG_TILES = BIG_WIDTH // INPROJ_TN


def _inproj_kernel(h_ref, wt_ref, o_ref, wb_ref):
    @pl.when(pl.program_id(1) == 0)
    def _():
        rows = lax.broadcasted_iota(jnp.int32, (INPROJ_TN, 1), 0)
        feat = pl.program_id(0) * INPROJ_TN + rows
        halved = (feat < D_MODEL) | (feat >= 3 * D_MODEL)
        wb_ref[...] = (wt_ref[...] * jnp.where(halved, 0.5, 1.0)).astype(BF16)

    o_ref[...] = lax.dot_general(h_ref[...], wb_ref[...], NT_DIMS,
                                 preferred_element_type=F32).astype(o_ref.dtype)


def _inproj(h, w_t):
    tm, tn = 1024, INPROJ_TN
    s = h.shape[0]
    sub = 8
    w_spec = pl.BlockSpec((pl.Element(tn), pl.Element(D_MODEL)),
                          lambda j, i: ((N_LAT // sub + j * (tn // sub)) * sub, 0))
    return pl.pallas_call(
        _inproj_kernel,
        grid=(N_BIG_TILES, s // tm),
        in_specs=[pl.BlockSpec((tm, D_MODEL), lambda j, i: (i, 0)), w_spec],
        out_specs=pl.BlockSpec((tm, tn), lambda j, i: (i, j)),
        out_shape=jax.ShapeDtypeStruct((s, BIG_WIDTH), BF16),
        scratch_shapes=[pltpu.VMEM((tn, D_MODEL), BF16)],
        compiler_params=pltpu.CompilerParams(
            dimension_semantics=("arbitrary", "arbitrary"), vmem_limit_bytes=VMEM_LIMIT),
        name="inproj",
    )(h, w_t)


ONES_ROWS = 16


def _flash_kernel(q_ref, k_ref, vt_ref, o_ref, s0_ref, s1_ref, m_ref, acc_ref, *, tq, tk):
    i = pl.program_id(1)
    n_diag = tq // tk
    bufs = (s0_ref, s1_ref)
    assert n_diag % 2 == 0

    def scores_into(s_ref, j, c0=0):
        start = pl.multiple_of(j * tk, tk)
        s_ref[:, c0:] = lax.dot_general(k_ref[pl.ds(start, tk), :], q_ref[c0:, :], NT_DIMS,
                                        preferred_element_type=F32)

    def update(s_ref, j, c0=0, causal=False):
        v_ext = jnp.concatenate([vt_ref[0, j], jnp.ones((ONES_ROWS, tk), BF16)], axis=0)
        for c in range(c0, tq, tk):
            s = s_ref[:, c:c + tk]
            if causal and c == c0:
                kv = lax.broadcasted_iota(jnp.int32, (tk, tk), 0)
                qq = lax.broadcasted_iota(jnp.int32, (tk, tk), 1)
                s = jnp.where(kv <= qq, s, -jnp.inf)
            m_prev = m_ref[:, c:c + tk]
            m_new = jnp.maximum(m_prev, jnp.max(s, axis=0, keepdims=True))
            alpha = jnp.exp2(m_prev - m_new)
            p = jnp.exp2(s - m_new).astype(BF16)
            pv = jnp.dot(v_ext, p, preferred_element_type=F32)
            acc_ref[:, c:c + tk] = alpha * acc_ref[:, c:c + tk] + pv
            m_ref[:, c:c + tk] = m_new

    m_ref[...] = jnp.full(m_ref.shape, -jnp.inf, F32)
    acc_ref[...] = jnp.zeros(acc_ref.shape, F32)
    scores_into(s0_ref, 0)

    def full_blocks(t, carry):
        for u in range(n_diag):
            scores_into(bufs[(u + 1) % 2], n_diag * t + u + 1)
            update(bufs[u % 2], n_diag * t + u)
        return carry

    lax.fori_loop(0, i, full_blocks, 0)

    for u in range(n_diag):
        if u + 1 < n_diag:
            scores_into(bufs[(u + 1) % 2], n_diag * i + u + 1, c0=(u + 1) * tk)
        update(bufs[u % 2], n_diag * i + u, c0=u * tk, causal=True)

    o_ref[...] = (acc_ref[:VDIM, :] / acc_ref[VDIM:VDIM + 1, :]).T.astype(o_ref.dtype)


def _flash(q, k, vt):
    tq, tk = 2048, KV_BLOCK
    s = q.shape[0]
    return pl.pallas_call(
        functools.partial(_flash_kernel, tq=tq, tk=tk),
        grid=(HEADS, s // tq),
        in_specs=[pl.BlockSpec((tq, HEAD_SLAB), lambda h, i: (i, h)),
                  pl.BlockSpec((s, HEAD_SLAB), lambda h, i: (0, h)),
                  pl.BlockSpec((1, s // tk, VDIM, tk), lambda h, i: (h, 0, 0, 0))],
        out_specs=pl.BlockSpec((tq, VDIM), lambda h, i: (i, h)),
        out_shape=jax.ShapeDtypeStruct((s, HEADS * VDIM), BF16),
        scratch_shapes=[pltpu.VMEM((tk, tq), F32), pltpu.VMEM((tk, tq), F32),
                        pltpu.VMEM((1, tq), F32), pltpu.VMEM((VDIM + ONES_ROWS, tq), F32)],
        compiler_params=pltpu.CompilerParams(
            dimension_semantics=("parallel", "arbitrary"), vmem_limit_bytes=VMEM_LIMIT),
        name="flash",
    )(q, k, vt)


OUT_COLS = 512


def _out_kernel(a_ref, zm_ref, u_ref, v_ref, zs_ref, gm_ref, gs_ref, x_ref, gate_ref,
                sg_ref, ws_ref, bs_ref, wo_ref, fg_ref, o_ref, mix_ref, y_ref, m0_ref, m1_ref,
                *, tm):
    i = pl.program_id(0)
    n_chunks = tm // CHUNK
    groups_per_cols = OUT_COLS // CHUNK

    @pl.when(i == 0)
    def _():
        m1_ref[...] = jnp.zeros(m1_ref.shape, m1_ref.dtype)

    rows = lax.broadcasted_iota(jnp.int32, (CHUNK, CHUNK), 0)
    cols = lax.broadcasted_iota(jnp.int32, (CHUNK, CHUNK), 1)
    causal = cols <= rows

    def gating(g, z):
        return ((jnp.tanh(g) + 1.0) * (jnp.tanh(z) + 1.0) * z).astype(F32)

    def step(prev_ref, next_ref):
        v = v_ref[...].astype(F32)
        v_scale = lax.rsqrt(jnp.mean(v * v, axis=-1, keepdims=True) + EPS)
        for cc in range(D_MODEL // OUT_COLS):
            sl = slice(cc * OUT_COLS, (cc + 1) * OUT_COLS)
            y_ref[:, sl] = jnp.dot(prev_ref[...], wo_ref[:, sl], preferred_element_type=F32)

            vn = (v_ref[:, sl].astype(F32) * v_scale * sg_ref[:, sl]).astype(BF16)
            for gg in range(groups_per_cols):
                g = cc * groups_per_cols + gg
                col = slice(g * CHUNK, (g + 1) * CHUNK)
                w = jnp.where(causal, ws_ref[g], 0.0).astype(BF16)
                rhs = jnp.concatenate(
                    [vn[c * CHUNK:(c + 1) * CHUNK, gg * CHUNK:(gg + 1) * CHUNK]
                     for c in range(n_chunks)], axis=1)
                mixed = jnp.dot(w, rhs, preferred_element_type=F32)
                for c in range(n_chunks):
                    mix_ref[c * CHUNK:(c + 1) * CHUNK, col] = (
                        mixed[:, c * CHUNK:(c + 1) * CHUNK] + bs_ref[:, col])
            y_mla = gating(gm_ref[:, sl], zm_ref[:, sl]) * a_ref[:, sl].astype(F32)
            y_sgu = (gating(gs_ref[:, sl], zs_ref[:, sl]) * u_ref[:, sl].astype(F32)
                     * mix_ref[:, sl])
            next_ref[:, sl] = (y_mla + y_sgu).astype(BF16)

        o_ref[...] = _rms(x_ref[...] + (0.5 * gate_ref[...]) * y_ref[...], fg_ref[...])

    @pl.when(i % 2 == 0)
    def _():
        step(m1_ref, m0_ref)

    @pl.when(i % 2 == 1)
    def _():
        step(m0_ref, m1_ref)


def _outproj(attn, proj, x2, gate, sg, ws, bs_full, wo, fg):
    tm = 256
    s = x2.shape[0]
    n = s // tm
    cur = lambda i: (jnp.minimum(i, n - 1), 0)
    prev = lambda i: (jnp.maximum(i - 1, 0), 0)
    big = lambda k: pl.BlockSpec((tm, D_MODEL), lambda i, k=k: (jnp.minimum(i, n - 1), k))
    return pl.pallas_call(
        functools.partial(_out_kernel, tm=tm),
        grid=(n + 1,),
        in_specs=[pl.BlockSpec((tm, D_MODEL), cur),
                  big(0), big(1), big(2), big(3), big(4), big(5),
                  pl.BlockSpec((tm, D_MODEL), prev), _resident(gate.shape), _resident(sg.shape),
                  _resident(ws.shape), _resident(bs_full.shape), _resident(wo.shape),
                  _resident(fg.shape)],
        out_specs=pl.BlockSpec((tm, D_MODEL), prev),
        out_shape=jax.ShapeDtypeStruct((s, D_MODEL), F32),
        scratch_shapes=[pltpu.VMEM((tm, D_MODEL), F32), pltpu.VMEM((tm, D_MODEL), F32),
                        pltpu.VMEM((tm, D_MODEL), BF16), pltpu.VMEM((tm, D_MODEL), BF16)],
        compiler_params=pltpu.CompilerParams(
            dimension_semantics=("arbitrary",), vmem_limit_bytes=VMEM_LIMIT),
        name="outproj",
    )(attn, proj, proj, proj, proj, proj, proj, x2, gate, sg, ws, bs_full, wo, fg)


def _rope_tables(positions):
    inv_freq = 1.0 / (ROPE_THETA ** (jnp.arange(0, ROPE, 2, dtype=F32) / ROPE))
    ang = positions.astype(F32)[:, None] * jnp.tile(inv_freq, LANE // (ROPE // 2))
    sin = jnp.sin(ang)
    first = (jnp.arange(LANE) % ROPE) < ROPE // 2
    return jnp.cos(ang), jnp.where(first, -sin, 0.0), jnp.where(first, 0.0, sin)


def kernel(x, c, positions, attn_norm_g, w_ada, b_ada, w_in, q_norm_g, w_uq, kv_norm_g, w_ukv,
           sgu_norm_g, w_spatial, b_spatial, w_out, final_norm_g):
    b, s, d = x.shape
    assert (b, s, d) == (1, SEQ, D_MODEL) and attn_norm_g.shape[0] == 1
    x2 = x.reshape(s, d)

    mod = _adaln(jnp.broadcast_to(c, (8, d)), w_ada[0], b_ada[0].reshape(1, 3 * d))[0:1]
    shift, scale, gate = mod[:, :d], mod[:, d:2 * d], mod[:, 2 * d:]

    w_t = jnp.swapaxes(w_in[0], 0, 1)
    wlat = w_t[:LAT_WIDTH].astype(BF16)
    wq = w_uq[0].reshape(Q_RANK, HEADS, NOPE + ROPE)
    wqn = wq[:, :, :NOPE].reshape(Q_RANK, HEADS * NOPE).astype(BF16)
    wqr = wq[:, :, NOPE:].reshape(Q_RANK, HEADS * ROPE).astype(BF16)
    wkv = w_ukv[0].reshape(KV_RANK, HEADS, NOPE + VDIM)
    wk = wkv[:, :, :NOPE].reshape(KV_RANK, HEADS * NOPE).astype(BF16)
    wvt = wkv[:, :, NOPE:].reshape(KV_RANK, HEADS * VDIM).T.astype(BF16)
    h, q, k, vt = _front(x2, attn_norm_g[0].reshape(1, d), scale, shift, wlat,
                         q_norm_g[0].reshape(1, Q_RANK), kv_norm_g[0].reshape(1, KV_RANK),
                         wqn, wqr, wk, wvt, *_rope_tables(positions[0]))

    proj = _inproj(h, w_t)
    attn = _flash(q, k, vt)

    bs_full = jnp.repeat(b_spatial[0].T, CHUNK, axis=1)
    out = _outproj(attn, proj, x2, gate, sgu_norm_g[0].reshape(1, d), w_spatial[0], bs_full,
                   w_out[0].astype(BF16), final_norm_g.reshape(1, d))
    return out.reshape(b, s, d)
```

```python
import functools
import math

import jax
import jax.numpy as jnp
from jax import lax
from jax.experimental import pallas as pl
from jax.experimental.pallas import tpu as pltpu

D_MODEL = 2048
SEQ = 8192
HEADS = 16
NOPE = 128
ROPE = 64
VDIM = 128
Q_RANK = 768
KV_RANK = 512
ROPE_THETA = 10000.0
GROUPS = 16
CHUNK = 128
EPS = 1e-6

LANE = 128
HEAD_SLAB = 2 * LANE
KV_BLOCK = 512
N_LAT = Q_RANK + KV_RANK + ROPE
LAT_WIDTH = N_LAT + ROPE
BIG_WIDTH = 6 * D_MODEL
VMEM_LIMIT = 56 * 1024 * 1024

Q_SCALE = (1.0 / math.sqrt(NOPE + ROPE)) * math.log2(math.e)

BF16 = jnp.bfloat16
F32 = jnp.float32
NT_DIMS = (((1,), (1,)), ((), ()))


def _sigmoid(x):
    return 0.5 * jnp.tanh(0.5 * x) + 0.5


def _rms(x, g):
    ms = jnp.mean(x * x, axis=-1, keepdims=True)
    return x * lax.rsqrt(ms + EPS) * g


def _resident(shape):
    return pl.BlockSpec(shape, lambda *_: (0,) * len(shape), pipeline_mode=pl.Buffered(1))


def _adaln_kernel(c_ref, w_ref, b_ref, o_ref):
    c = c_ref[...]
    c_act = c * _sigmoid(c)
    o_ref[...] = jnp.dot(c_act.astype(BF16), w_ref[...].astype(BF16),
                         preferred_element_type=F32) + b_ref[...]


def _adaln(c8, w_ada, b_ada):
    tn = 1536
    n = w_ada.shape[1]
    return pl.pallas_call(
        _adaln_kernel,
        grid=(n // tn,),
        in_specs=[pl.BlockSpec((8, D_MODEL), lambda j: (0, 0)),
                  pl.BlockSpec((D_MODEL, tn), lambda j: (0, j)),
                  pl.BlockSpec((1, tn), lambda j: (0, j))],
        out_specs=pl.BlockSpec((8, tn), lambda j: (0, j)),
        out_shape=jax.ShapeDtypeStruct((8, n), F32),
        compiler_params=pltpu.CompilerParams(
            dimension_semantics=("arbitrary",), vmem_limit_bytes=VMEM_LIMIT),
        name="adaln",
    )(c8, w_ada, b_ada)


def _rope(t, cos4, sin_a, sin_b):
    return t * cos4 + pltpu.roll(t, 96, 1) * sin_a + pltpu.roll(t, 32, 1) * sin_b


def _front_kernel(x_ref, g_ref, scale_ref, shift_ref, wlat_ref, qg_ref, kvg_ref, wqn_ref,
                  wqr_ref, wk_ref, wvt_ref, cos_ref, sa_ref, sb_ref, h_ref, q_ref, k_ref, vt_ref,
                  wlat_bf_ref):
    @pl.when(pl.program_id(0) == 0)
    def _():
        wlat_bf_ref[...] = wlat_ref[...].astype(BF16)

    y = _rms(x_ref[...], g_ref[...])
    h = (y * (1.0 + scale_ref[...]) + shift_ref[...]).astype(BF16)
    h_ref[...] = h
    lat = lax.dot_general(h, wlat_bf_ref[...], NT_DIMS, preferred_element_type=F32)

    cos4, sin_a, sin_b = cos_ref[...], sa_ref[...], sb_ref[...]
    low_half = lax.broadcasted_iota(jnp.int32, cos4.shape, 1) < ROPE

    qn = _rms(lat[:, :Q_RANK], qg_ref[...]).astype(BF16)
    q_nope = jnp.dot(qn, wqn_ref[...], preferred_element_type=F32)
    q_rope = jnp.dot(qn, wqr_ref[...], preferred_element_type=F32)
    for pair in range(HEADS // 2):
        rot = _rope(q_rope[:, pair * LANE:(pair + 1) * LANE], cos4, sin_a, sin_b) * Q_SCALE
        for odd, slab in enumerate((rot, pltpu.roll(rot, ROPE, 1))):
            lo = (2 * pair + odd) * HEAD_SLAB
            q_ref[:, lo + LANE:lo + HEAD_SLAB] = jnp.where(low_half, slab, 0.0).astype(BF16)
    for hd in range(HEADS):
        lo = hd * HEAD_SLAB
        q_ref[:, lo:lo + LANE] = (q_nope[:, hd * LANE:(hd + 1) * LANE] * Q_SCALE).astype(BF16)

    kvn = _rms(lat[:, Q_RANK:Q_RANK + KV_RANK], kvg_ref[...]).astype(BF16)
    vt = lax.dot_general(wvt_ref[...], kvn, NT_DIMS, preferred_element_type=F32)
    vt_ref[...] = vt.astype(BF16).reshape(vt_ref.shape)
    kn = jnp.dot(kvn, wk_ref[...], preferred_element_type=F32)
    kr = _rope(lat[:, Q_RANK + KV_RANK:Q_RANK + KV_RANK + LANE], cos4, sin_a, sin_b)
    kr = jnp.where(low_half, kr, 0.0).astype(BF16)
    for hd in range(HEADS):
        lo = hd * HEAD_SLAB
        k_ref[:, lo:lo + LANE] = kn[:, hd * LANE:(hd + 1) * LANE].astype(BF16)
        k_ref[:, lo + LANE:lo + HEAD_SLAB] = kr


def _front(x2, g, scale, shift, w_t, qg, kvg, wqn, wqr, wk, wvt, cos4, sin_a, sin_b):
    tm = 256
    s = x2.shape[0]
    per_block = KV_BLOCK // tm
    row = lambda i: (i, 0)
    tab = pl.BlockSpec((tm, LANE), row)
    slab = pl.BlockSpec((tm, HEADS * HEAD_SLAB), row)
    vt_spec = pl.BlockSpec((HEADS, 1, VDIM, tm), lambda i: (0, i // per_block, 0, i % per_block))
    wlat_spec = pl.BlockSpec((pl.Element(LAT_WIDTH), pl.Element(D_MODEL)), lambda i: (0, 0),
                             pipeline_mode=pl.Buffered(1))
    return pl.pallas_call(
        _front_kernel,
        grid=(s // tm,),
        in_specs=[pl.BlockSpec((tm, D_MODEL), row),
                  _resident(g.shape), _resident(scale.shape), _resident(shift.shape),
                  wlat_spec, _resident(qg.shape), _resident(kvg.shape),
                  _resident(wqn.shape), _resident(wqr.shape), _resident(wk.shape),
                  _resident(wvt.shape), tab, tab, tab],
        out_specs=[pl.BlockSpec((tm, D_MODEL), row), slab, slab, vt_spec],
        out_shape=[jax.ShapeDtypeStruct((s, D_MODEL), BF16),
                   jax.ShapeDtypeStruct((s, HEADS * HEAD_SLAB), BF16),
                   jax.ShapeDtypeStruct((s, HEADS * HEAD_SLAB), BF16),
                   jax.ShapeDtypeStruct((HEADS, s // KV_BLOCK, VDIM, KV_BLOCK), BF16)],
        scratch_shapes=[pltpu.VMEM((LAT_WIDTH, D_MODEL), BF16)],
        compiler_params=pltpu.CompilerParams(
            dimension_semantics=("arbitrary",), vmem_limit_bytes=VMEM_LIMIT),
        name="front",
    )(x2, g, scale, shift, w_t, qg, kvg, wqn, wqr, wk, wvt, cos4, sin_a, sin_b)


INPROJ_TN = 1536
N_BIG_TILES = BIG_WIDTH // INPROJ_TN


def _inproj_kernel(h_ref, wt_ref, o_ref, wb_ref):
    @pl.when(pl.program_id(1) == 0)
    def _():
        rows = lax.broadcasted_iota(jnp.int32, (INPROJ_TN, 1), 0)
        feat = pl.program_id(0) * INPROJ_TN + rows
        halved = (feat < D_MODEL) | (feat >= 3 * D_MODEL)
        wb_ref[...] = (wt_ref[...] * jnp.where(halved, 0.5, 1.0)).astype(BF16)

    o_ref[...] = lax.dot_general(h_ref[...], wb_ref[...], NT_DIMS,
                                 preferred_element_type=F32).astype(o_ref.dtype)


def _inproj(h, w_t):
    tm, tn = 1024, INPROJ_TN
    s = h.shape[0]
    sub = 8
    w_spec = pl.BlockSpec((pl.Element(tn), pl.Element(D_MODEL)),
                          lambda j, i: ((N_LAT // sub + j * (tn // sub)) * sub, 0))
    return pl.pallas_call(
        _inproj_kernel,
        grid=(N_BIG_TILES, s // tm),
        in_specs=[pl.BlockSpec((tm, D_MODEL), lambda j, i: (i, 0)), w_spec],
        out_specs=pl.BlockSpec((tm, tn), lambda j, i: (i, j)),
        out_shape=jax.ShapeDtypeStruct((s, BIG_WIDTH), BF16),
        scratch_shapes=[pltpu.VMEM((tn, D_MODEL), BF16)],
        compiler_params=pltpu.CompilerParams(
            dimension_semantics=("arbitrary", "arbitrary"), vmem_limit_bytes=VMEM_LIMIT),
        name="inproj",
    )(h, w_t)


ONES_ROWS = 16


def _flash_kernel(q_ref, k_ref, vt_ref, o_ref, s0_ref, s1_ref, m_ref, acc_ref, *, tq, tk):
    i = pl.program_id(1)
    n_diag = tq // tk
    bufs = (s0_ref, s1_ref)
    assert n_diag % 2 == 0

    half = tk // 2

    def qk(k_rows, c_lo, c_hi):
        return lax.dot_general(k_rows, q_ref[c_lo:c_hi, :], NT_DIMS, preferred_element_type=F32)

    def scores_into(s_ref, j, c0=0, causal=False):
        k_blk = k_ref[pl.ds(pl.multiple_of(j * tk, tk), tk), :]
        for c in range(c0, tq, tk):
            if causal and c == c0:
                s_ref[:half, c:c + half] = qk(k_blk[:half], c, c + half)
                s_ref[:, c + half:c + tk] = qk(k_blk, c + half, c + tk)
            else:
                s_ref[:, c:c + tk] = qk(k_blk, c, c + tk)

    def fold(s, v_rows, cols):
        m_prev = m_ref[:, cols]
        m_new = jnp.maximum(m_prev, jnp.max(s, axis=0, keepdims=True))
        alpha = jnp.exp2(m_prev - m_new)
        p = jnp.exp2(s - m_new).astype(BF16)
        pv = jnp.dot(v_rows, p, preferred_element_type=F32)
        acc_ref[:, cols] = alpha * acc_ref[:, cols] + pv
        m_ref[:, cols] = m_new

    def update(s_ref, j, c0=0, causal=False):
        v_ext = jnp.concatenate([vt_ref[0, j], jnp.ones((ONES_ROWS, tk), BF16)], axis=0)
        for c in range(c0, tq, tk):
            if causal and c == c0:
                kv = lax.broadcasted_iota(jnp.int32, (half, half), 0)
                qq = lax.broadcasted_iota(jnp.int32, (half, half), 1)
                tri = kv <= qq
                fold(jnp.where(tri, s_ref[:half, c:c + half], -jnp.inf), v_ext[:, :half],
                     slice(c, c + half))
                late = jnp.where(tri, s_ref[half:, c + half:c + tk], -jnp.inf)
                fold(jnp.concatenate([s_ref[:half, c + half:c + tk], late], axis=0), v_ext,
                     slice(c + half, c + tk))
            else:
                fold(s_ref[:, c:c + tk], v_ext, slice(c, c + tk))

    m_ref[...] = jnp.full(m_ref.shape, -jnp.inf, F32)
    acc_ref[...] = jnp.zeros(acc_ref.shape, F32)
    scores_into(s0_ref, 0)

    def full_blocks(t, carry):
        for u in range(n_diag):
            scores_into(bufs[(u + 1) % 2], n_diag * t + u + 1)
            update(bufs[u % 2], n_diag * t + u)
        return carry

    lax.fori_loop(0, i, full_blocks, 0)

    for u in range(n_diag):
        if u + 1 < n_diag:
            scores_into(bufs[(u + 1) % 2], n_diag * i + u + 1, c0=(u + 1) * tk, causal=True)
        update(bufs[u % 2], n_diag * i + u, c0=u * tk, causal=True)

    o_ref[...] = (acc_ref[:VDIM, :] / acc_ref[VDIM:VDIM + 1, :]).T.astype(o_ref.dtype)


def _flash(q, k, vt):
    tq, tk = 2048, KV_BLOCK
    s = q.shape[0]
    return pl.pallas_call(
        functools.partial(_flash_kernel, tq=tq, tk=tk),
        grid=(HEADS, s // tq),
        in_specs=[pl.BlockSpec((tq, HEAD_SLAB), lambda h, i: (i, h)),
                  pl.BlockSpec((s, HEAD_SLAB), lambda h, i: (0, h)),
                  pl.BlockSpec((1, s // tk, VDIM, tk), lambda h, i: (h, 0, 0, 0))],
        out_specs=pl.BlockSpec((tq, VDIM), lambda h, i: (i, h)),
        out_shape=jax.ShapeDtypeStruct((s, HEADS * VDIM), BF16),
        scratch_shapes=[pltpu.VMEM((tk, tq), F32), pltpu.VMEM((tk, tq), F32),
                        pltpu.VMEM((1, tq), F32), pltpu.VMEM((VDIM + ONES_ROWS, tq), F32)],
        compiler_params=pltpu.CompilerParams(
            dimension_semantics=("parallel", "arbitrary"), vmem_limit_bytes=VMEM_LIMIT),
        name="flash",
    )(q, k, vt)


def _out_kernel(a_ref, zm_ref, u_ref, v_ref, zs_ref, gm_ref, gs_ref, x_ref, gate_ref,
                sg_ref, ws_ref, bs_ref, wo_ref, fg_ref, o_ref, mix_ref, wo_bf_ref, *, tm):
    @pl.when(pl.program_id(0) == 0)
    def _():
        wo_bf_ref[...] = wo_ref[...].astype(BF16)

    rows = lax.broadcasted_iota(jnp.int32, (CHUNK, CHUNK), 0)
    cols = lax.broadcasted_iota(jnp.int32, (CHUNK, CHUNK), 1)
    causal = cols <= rows
    n_chunks = tm // CHUNK

    vn = _rms(v_ref[...].astype(F32), sg_ref[...]).astype(BF16)
    for g in range(GROUPS):
        w = jnp.where(causal, ws_ref[g], 0.0).astype(BF16)
        col = slice(g * CHUNK, (g + 1) * CHUNK)
        rhs = jnp.concatenate([vn[c * CHUNK:(c + 1) * CHUNK, col] for c in range(n_chunks)],
                              axis=1)
        mixed = jnp.dot(w, rhs, preferred_element_type=F32)
        for c in range(n_chunks):
            mix_ref[c * CHUNK:(c + 1) * CHUNK, col] = (mixed[:, c * CHUNK:(c + 1) * CHUNK]
                                                       + bs_ref[:, col])

    def gating(g_ref, z_ref):
        z = z_ref[...]
        return ((jnp.tanh(g_ref[...]) + 1.0) * (jnp.tanh(z) + 1.0) * z).astype(F32)

    y_mla = gating(gm_ref, zm_ref) * a_ref[...].astype(F32)
    y_sgu = gating(gs_ref, zs_ref) * u_ref[...].astype(F32) * mix_ref[...]
    twice_merged = (y_mla + y_sgu).astype(BF16)
    y = jnp.dot(twice_merged, wo_bf_ref[...], preferred_element_type=F32)
    o_ref[...] = _rms(x_ref[...] + (0.5 * gate_ref[...]) * y, fg_ref[...])


def _outproj(attn, proj, x2, gate, sg, ws, bs_full, wo, fg):
    tm = 256
    s = x2.shape[0]
    row = lambda i: (i, 0)
    big = lambda k: pl.BlockSpec((tm, D_MODEL), lambda i, k=k: (i, k))
    return pl.pallas_call(
        functools.partial(_out_kernel, tm=tm),
        grid=(s // tm,),
        in_specs=[pl.BlockSpec((tm, D_MODEL), row),
                  big(0), big(1), big(2), big(3), big(4), big(5),
                  pl.BlockSpec((tm, D_MODEL), row), _resident(gate.shape), _resident(sg.shape),
                  _resident(ws.shape), _resident(bs_full.shape), _resident(wo.shape),
                  _resident(fg.shape)],
        out_specs=pl.BlockSpec((tm, D_MODEL), row),
        out_shape=jax.ShapeDtypeStruct((s, D_MODEL), F32),
        scratch_shapes=[pltpu.VMEM((tm, D_MODEL), F32), pltpu.VMEM((D_MODEL, D_MODEL), BF16)],
        compiler_params=pltpu.CompilerParams(
            dimension_semantics=("arbitrary",), vmem_limit_bytes=VMEM_LIMIT),
        name="outproj",
    )(attn, proj, proj, proj, proj, proj, proj, x2, gate, sg, ws, bs_full, wo, fg)


def _rope_tables(positions):
    inv_freq = 1.0 / (ROPE_THETA ** (jnp.arange(0, ROPE, 2, dtype=F32) / ROPE))
    ang = positions.astype(F32)[:, None] * jnp.tile(inv_freq, LANE // (ROPE // 2))
    sin = jnp.sin(ang)
    first = (jnp.arange(LANE) % ROPE) < ROPE // 2
    return jnp.cos(ang), jnp.where(first, -sin, 0.0), jnp.where(first, 0.0, sin)


def kernel(x, c, positions, attn_norm_g, w_ada, b_ada, w_in, q_norm_g, w_uq, kv_norm_g, w_ukv,
           sgu_norm_g, w_spatial, b_spatial, w_out, final_norm_g):
    b, s, d = x.shape
    assert (b, s, d) == (1, SEQ, D_MODEL) and attn_norm_g.shape[0] == 1
    x2 = x.reshape(s, d)

    mod = _adaln(jnp.broadcast_to(c, (8, d)), w_ada[0], b_ada[0].reshape(1, 3 * d))[0:1]
    shift, scale, gate = mod[:, :d], mod[:, d:2 * d], mod[:, 2 * d:]

    w_t = jnp.swapaxes(w_in[0], 0, 1)
    wq =w_uq[0].reshape(Q_RANK, HEADS, NOPE + ROPE)
    wqn = wq[:, :, :NOPE].reshape(Q_RANK, HEADS * NOPE).astype(BF16)
    wqr = wq[:, :, NOPE:].reshape(Q_RANK, HEADS * ROPE).astype(BF16)
    wkv = w_ukv[0].reshape(KV_RANK, HEADS, NOPE + VDIM)
    wk = wkv[:, :, :NOPE].reshape(KV_RANK, HEADS * NOPE).astype(BF16)
    wvt = wkv[:, :, NOPE:].reshape(KV_RANK, HEADS * VDIM).T.astype(BF16)
    h, q, k, vt = _front(x2, attn_norm_g[0].reshape(1, d), scale, shift, w_t,
                         q_norm_g[0].reshape(1, Q_RANK), kv_norm_g[0].reshape(1, KV_RANK),
                         wqn, wqr, wk, wvt, *_rope_tables(positions[0]))

    proj = _inproj(h, w_t)
    attn = _flash(q, k, vt)

    bs_full = jnp.repeat(b_spatial[0].T, CHUNK, axis=1)
    out = _outproj(attn, proj, x2, gate, sgu_norm_g[0].reshape(1, d), w_spatial[0], bs_full,
                   w_out[0], final_norm_g.reshape(1, d))
    return out.reshape(b, s, d)
```

```python
import functools
import math

import jax
import jax.numpy as jnp
from jax import lax
from jax.experimental import pallas as pl
from jax.experimental.pallas import tpu as pltpu

D_MODEL = 2048
SEQ = 8192
HEADS = 16
NOPE = 128
ROPE = 64
VDIM = 128
Q_RANK = 768
KV_RANK = 512
ROPE_THETA = 10000.0
GROUPS = 16
CHUNK = 128
EPS = 1e-6

LANE = 128
HEAD_SLAB = 2 * LANE
KV_BLOCK = 512
N_LAT = Q_RANK + KV_RANK + ROPE
LAT_WIDTH = N_LAT + ROPE
BIG_WIDTH = 6 * D_MODEL
VMEM_LIMIT = 56 * 1024 * 1024

Q_SCALE = (1.0 / math.sqrt(NOPE + ROPE)) * math.log2(math.e)

BF16 = jnp.bfloat16
F32 = jnp.float32
NT_DIMS = (((1,), (1,)), ((), ()))


def _sigmoid(x):
    return 0.5 * jnp.tanh(0.5 * x) + 0.5


def _rms(x, g):
    ms = jnp.mean(x * x, axis=-1, keepdims=True)
    return x * lax.rsqrt(ms + EPS) * g


def _resident(shape):
    return pl.BlockSpec(shape, lambda *_: (0,) * len(shape), pipeline_mode=pl.Buffered(1))


def _adaln_kernel(c_ref, w_ref, b_ref, o_ref):
    c = c_ref[...]
    c_act = c * _sigmoid(c)
    o_ref[...] = jnp.dot(c_act.astype(BF16), w_ref[...].astype(BF16),
                         preferred_element_type=F32) + b_ref[...]


def _adaln(c8, w_ada, b_ada):
    tn = 1536
    n = w_ada.shape[1]
    return pl.pallas_call(
        _adaln_kernel,
        grid=(n // tn,),
        in_specs=[pl.BlockSpec((8, D_MODEL), lambda j: (0, 0)),
                  pl.BlockSpec((D_MODEL, tn), lambda j: (0, j)),
                  pl.BlockSpec((1, tn), lambda j: (0, j))],
        out_specs=pl.BlockSpec((8, tn), lambda j: (0, j)),
        out_shape=jax.ShapeDtypeStruct((8, n), F32),
        compiler_params=pltpu.CompilerParams(
            dimension_semantics=("arbitrary",), vmem_limit_bytes=VMEM_LIMIT),
        name="adaln",
    )(c8, w_ada, b_ada)


def _regroup_kernel(wuq_ref, wukv_ref, wqn_ref, wqr_ref, wk_ref, wvt_ref):
    x0, x1, x2 = (wuq_ref[:, k * LANE:(k + 1) * LANE] for k in range(3))
    first = lax.broadcasted_iota(jnp.int32, x0.shape, 1) < ROPE
    nope_b = jnp.where(first, pltpu.roll(x1, ROPE, 1), pltpu.roll(x2, ROPE, 1))
    wqn_ref[:, :LANE] = x0.astype(BF16)
    wqn_ref[:, LANE:] = nope_b.astype(BF16)
    wqr_ref[...] = jnp.where(first, x1, x2).astype(BF16)
    for hd in range(2):
        lo = hd * (NOPE + VDIM)
        wk_ref[:, hd * NOPE:(hd + 1) * NOPE] = wukv_ref[:, lo:lo + NOPE].astype(BF16)
        wvt_ref[hd * VDIM:(hd + 1) * VDIM, :] = (
            wukv_ref[:, lo + NOPE:lo + NOPE + VDIM].T.astype(BF16))


def _regroup_up_weights(w_uq, w_ukv):
    pairs = HEADS // 2
    col = lambda p: (0, p)
    return pl.pallas_call(
        _regroup_kernel,
        grid=(pairs,),
        in_specs=[pl.BlockSpec((Q_RANK, 2 * (NOPE + ROPE)), col),
                  pl.BlockSpec((KV_RANK, 2 * (NOPE + VDIM)), col)],
        out_specs=[pl.BlockSpec((Q_RANK, 2 * NOPE), col), pl.BlockSpec((Q_RANK, 2 * ROPE), col),
                   pl.BlockSpec((KV_RANK, 2 * NOPE), col),
                   pl.BlockSpec((2 * VDIM, KV_RANK), lambda p: (p, 0))],
        out_shape=[jax.ShapeDtypeStruct((Q_RANK, HEADS * NOPE), BF16),
                   jax.ShapeDtypeStruct((Q_RANK, HEADS * ROPE), BF16),
                   jax.ShapeDtypeStruct((KV_RANK, HEADS * NOPE), BF16),
                   jax.ShapeDtypeStruct((HEADS * VDIM, KV_RANK), BF16)],
        compiler_params=pltpu.CompilerParams(
            dimension_semantics=("parallel",), vmem_limit_bytes=VMEM_LIMIT),
        name="regroup",
    )(w_uq, w_ukv)


def _rope(t, cos4, sin_a, sin_b):
    return t * cos4 + pltpu.roll(t, 96, 1) * sin_a + pltpu.roll(t, 32, 1) * sin_b


def _front_kernel(x_ref, g_ref, scale_ref, shift_ref, wlat_ref, qg_ref, kvg_ref, wqn_ref,
                  wqr_ref, wk_ref, wvt_ref, cos_ref, sa_ref, sb_ref, h_ref, q_ref, k_ref, vt_ref,
                  wlat_bf_ref):
    @pl.when(pl.program_id(0) == 0)
    def _():
        wlat_bf_ref[...] = wlat_ref[...].astype(BF16)

    y = _rms(x_ref[...], g_ref[...])
    h = (y * (1.0 + scale_ref[...]) + shift_ref[...]).astype(BF16)
    h_ref[...] = h
    lat = lax.dot_general(h, wlat_bf_ref[...], NT_DIMS, preferred_element_type=F32)

    cos4, sin_a, sin_b = cos_ref[...], sa_ref[...], sb_ref[...]
    low_half = lax.broadcasted_iota(jnp.int32, cos4.shape, 1) < ROPE

    qn = _rms(lat[:, :Q_RANK], qg_ref[...]).astype(BF16)
    q_nope = jnp.dot(qn, wqn_ref[...], preferred_element_type=F32)
    q_rope = jnp.dot(qn, wqr_ref[...], preferred_element_type=F32)
    for pair in range(HEADS // 2):
        rot = _rope(q_rope[:, pair * LANE:(pair + 1) * LANE], cos4, sin_a, sin_b) * Q_SCALE
        for odd, slab in enumerate((rot, pltpu.roll(rot, ROPE, 1))):
            lo = (2 * pair + odd) * HEAD_SLAB
            q_ref[:, lo + LANE:lo + HEAD_SLAB] = jnp.where(low_half, slab, 0.0).astype(BF16)
    for hd in range(HEADS):
        lo = hd * HEAD_SLAB
        q_ref[:, lo:lo + LANE] = (q_nope[:, hd * LANE:(hd + 1) * LANE] * Q_SCALE).astype(BF16)

    kvn = _rms(lat[:, Q_RANK:Q_RANK + KV_RANK], kvg_ref[...]).astype(BF16)
    vt = lax.dot_general(wvt_ref[...], kvn, NT_DIMS, preferred_element_type=F32)
    vt_ref[...] = vt.astype(BF16).reshape(vt_ref.shape)
    kn = jnp.dot(kvn, wk_ref[...], preferred_element_type=F32)
    kr = _rope(lat[:, Q_RANK + KV_RANK:Q_RANK + KV_RANK + LANE], cos4, sin_a, sin_b)
    kr = jnp.where(low_half, kr, 0.0).astype(BF16)
    for hd in range(HEADS):
        lo = hd * HEAD_SLAB
        k_ref[:, lo:lo + LANE] = kn[:, hd * LANE:(hd + 1) * LANE].astype(BF16)
        k_ref[:, lo + LANE:lo + HEAD_SLAB] = kr


def _front(x2, g, scale, shift, w_t, qg, kvg, wqn, wqr, wk, wvt, cos4, sin_a, sin_b):
    tm = 256
    s = x2.shape[0]
    per_block = KV_BLOCK // tm
    row = lambda i: (i, 0)
    tab = pl.BlockSpec((tm, LANE), row)
    slab = pl.BlockSpec((tm, HEADS * HEAD_SLAB), row)
    vt_spec = pl.BlockSpec((HEADS, 1, VDIM, tm), lambda i: (0, i // per_block, 0, i % per_block))
    wlat_spec = pl.BlockSpec((pl.Element(LAT_WIDTH), pl.Element(D_MODEL)), lambda i: (0, 0),
                             pipeline_mode=pl.Buffered(1))
    return pl.pallas_call(
        _front_kernel,
        grid=(s // tm,),
        in_specs=[pl.BlockSpec((tm, D_MODEL), row),
                  _resident(g.shape), _resident(scale.shape), _resident(shift.shape),
                  wlat_spec, _resident(qg.shape), _resident(kvg.shape),
                  _resident(wqn.shape), _resident(wqr.shape), _resident(wk.shape),
                  _resident(wvt.shape), tab, tab, tab],
        out_specs=[pl.BlockSpec((tm, D_MODEL), row), slab, slab, vt_spec],
        out_shape=[jax.ShapeDtypeStruct((s, D_MODEL), BF16),
                   jax.ShapeDtypeStruct((s, HEADS * HEAD_SLAB), BF16),
                   jax.ShapeDtypeStruct((s, HEADS * HEAD_SLAB), BF16),
                   jax.ShapeDtypeStruct((HEADS, s // KV_BLOCK, VDIM, KV_BLOCK), BF16)],
        scratch_shapes=[pltpu.VMEM((LAT_WIDTH, D_MODEL), BF16)],
        compiler_params=pltpu.CompilerParams(
            dimension_semantics=("arbitrary",), vmem_limit_bytes=VMEM_LIMIT),
        name="front",
    )(x2, g, scale, shift, w_t, qg, kvg, wqn, wqr, wk, wvt, cos4, sin_a, sin_b)


INPROJ_TN = 1536
N_BIG_TILES = BIG_WIDTH // INPROJ_TN


def _inproj_kernel(h_ref, wt_ref, o_ref, wb_ref):
    @pl.when(pl.program_id(1) == 0)
    def _():
        rows = lax.broadcasted_iota(jnp.int32, (INPROJ_TN, 1), 0)
        feat = pl.program_id(0) * INPROJ_TN + rows
        halved = (feat < D_MODEL) | (feat >= 3 * D_MODEL)
        wb_ref[...] = (wt_ref[...] * jnp.where(halved, 0.5, 1.0)).astype(BF16)

    o_ref[...] = lax.dot_general(h_ref[...], wb_ref[...], NT_DIMS,
                                 preferred_element_type=F32).astype(o_ref.dtype)


def _inproj(h, w_t):
    tm, tn = 1024, INPROJ_TN
    s = h.shape[0]
    sub = 8
    w_spec = pl.BlockSpec((pl.Element(tn), pl.Element(D_MODEL)),
                          lambda j, i: ((N_LAT // sub + j * (tn // sub)) * sub, 0))
    return pl.pallas_call(
        _inproj_kernel,
        grid=(N_BIG_TILES, s // tm),
        in_specs=[pl.BlockSpec((tm, D_MODEL), lambda j, i: (i, 0)), w_spec],
        out_specs=pl.BlockSpec((tm, tn), lambda j, i: (i, j)),
        out_shape=jax.ShapeDtypeStruct((s, BIG_WIDTH), BF16),
        scratch_shapes=[pltpu.VMEM((tn, D_MODEL), BF16)],
        compiler_params=pltpu.CompilerParams(
            dimension_semantics=("arbitrary", "arbitrary"), vmem_limit_bytes=VMEM_LIMIT),
        name="inproj",
    )(h, w_t)


ONES_ROWS = 16


def _flash_kernel(q_ref, k_ref, vt_ref, o_ref, s0_ref, s1_ref, m_ref, acc_ref, *, tq, tk):
    i = pl.program_id(1)
    n_diag = tq // tk
    bufs = (s0_ref, s1_ref)
    assert n_diag % 2 == 0

    half = tk // 2

    def qk(k_rows, c_lo, c_hi):
        return lax.dot_general(k_rows, q_ref[c_lo:c_hi, :], NT_DIMS, preferred_element_type=F32)

    def scores_into(s_ref, j, c0=0, causal=False):
        k_blk = k_ref[pl.ds(pl.multiple_of(j * tk, tk), tk), :]
        for c in range(c0, tq, tk):
            if causal and c == c0:
                s_ref[:half, c:c + half] = qk(k_blk[:half], c, c + half)
                s_ref[:, c + half:c + tk] = qk(k_blk, c + half, c + tk)
            else:
                s_ref[:, c:c + tk] = qk(k_blk, c, c + tk)

    def fold(s, v_rows, cols):
        m_prev = m_ref[:, cols]
        m_new = jnp.maximum(m_prev, jnp.max(s, axis=0, keepdims=True))
        alpha = jnp.exp2(m_prev - m_new)
        p = jnp.exp2(s - m_new).astype(BF16)
        pv = jnp.dot(v_rows, p, preferred_element_type=F32)
        acc_ref[:, cols] = alpha * acc_ref[:, cols] + pv
        m_ref[:, cols] = m_new

    def update(s_ref, j, c0=0, causal=False):
        v_ext = jnp.concatenate([vt_ref[0, j], jnp.ones((ONES_ROWS, tk), BF16)], axis=0)
        for c in range(c0, tq, tk):
            if causal and c == c0:
                kv = lax.broadcasted_iota(jnp.int32, (half, half), 0)
                qq = lax.broadcasted_iota(jnp.int32, (half, half), 1)
                tri = kv <= qq
                fold(jnp.where(tri, s_ref[:half, c:c + half], -jnp.inf), v_ext[:, :half],
                     slice(c, c + half))
                late = jnp.where(tri, s_ref[half:, c + half:c + tk], -jnp.inf)
                fold(jnp.concatenate([s_ref[:half, c + half:c + tk], late], axis=0), v_ext,
                     slice(c + half, c + tk))
            else:
                fold(s_ref[:, c:c + tk], v_ext, slice(c, c + tk))

    m_ref[...] = jnp.full(m_ref.shape, -jnp.inf, F32)
    acc_ref[...] = jnp.zeros(acc_ref.shape, F32)
    scores_into(s0_ref, 0)

    def full_blocks(t, carry):
        for u in range(n_diag):
            nxt, cur = n_diag * t + u + 1, n_diag * t + u
            k_blk = k_ref[pl.ds(pl.multiple_of(nxt * tk, tk), tk), :]
            v_ext = jnp.concatenate([vt_ref[0, cur], jnp.ones((ONES_ROWS, tk), BF16)], axis=0)
            for c in range(0, tq, tk):
                bufs[(u + 1) % 2][:, c:c + tk] = qk(k_blk, c, c + tk)
                fold(bufs[u % 2][:, c:c + tk], v_ext, slice(c, c + tk))
        return carry

    lax.fori_loop(0, i, full_blocks, 0)

    for u in range(n_diag):
        if u + 1 < n_diag:
            scores_into(bufs[(u + 1) % 2], n_diag * i + u + 1, c0=(u + 1) * tk, causal=True)
        update(bufs[u % 2], n_diag * i + u, c0=u * tk, causal=True)

    o_ref[...] = (acc_ref[:VDIM, :] / acc_ref[VDIM:VDIM + 1, :]).T.astype(o_ref.dtype)


def _flash(q, k, vt):
    tq, tk = 2048, KV_BLOCK
    s = q.shape[0]
    return pl.pallas_call(
        functools.partial(_flash_kernel, tq=tq, tk=tk),
        grid=(HEADS, s // tq),
        in_specs=[pl.BlockSpec((tq, HEAD_SLAB), lambda h, i: (i, h)),
                  pl.BlockSpec((s, HEAD_SLAB), lambda h, i: (0, h)),
                  pl.BlockSpec((1, s // tk, VDIM, tk), lambda h, i: (h, 0, 0, 0))],
        out_specs=pl.BlockSpec((tq, VDIM), lambda h, i: (i, h)),
        out_shape=jax.ShapeDtypeStruct((s, HEADS * VDIM), BF16),
        scratch_shapes=[pltpu.VMEM((tk, tq), F32), pltpu.VMEM((tk, tq), F32),
                        pltpu.VMEM((1, tq), F32), pltpu.VMEM((VDIM + ONES_ROWS, tq), F32)],
        compiler_params=pltpu.CompilerParams(
            dimension_semantics=("parallel", "arbitrary"), vmem_limit_bytes=VMEM_LIMIT),
        name="flash",
    )(q, k, vt)


def _out_kernel(a_ref, zm_ref, u_ref, v_ref, zs_ref, gm_ref, gs_ref, x_ref, gate_ref,
                sg_ref, ws_ref, bs_ref, wo_ref, fg_ref, o_ref, mix_ref, wo_bf_ref, *, tm):
    @pl.when(pl.program_id(0) == 0)
    def _():
        wo_bf_ref[...] = wo_ref[...].astype(BF16)

    rows = lax.broadcasted_iota(jnp.int32, (CHUNK, CHUNK), 0)
    cols = lax.broadcasted_iota(jnp.int32, (CHUNK, CHUNK), 1)
    causal = cols <= rows
    n_chunks = tm // CHUNK

    vn = _rms(v_ref[...].astype(F32), sg_ref[...]).astype(BF16)
    for g in range(GROUPS):
        w = jnp.where(causal, ws_ref[g], 0.0).astype(BF16)
        col = slice(g * CHUNK, (g + 1) * CHUNK)
        rhs = jnp.concatenate([vn[c * CHUNK:(c + 1) * CHUNK, col] for c in range(n_chunks)],
                              axis=1)
        mixed = jnp.dot(w, rhs, preferred_element_type=F32)
        for c in range(n_chunks):
            mix_ref[c * CHUNK:(c + 1) * CHUNK, col] = (mixed[:, c * CHUNK:(c + 1) * CHUNK]
                                                       + bs_ref[:, col])

    def gating(g_ref, z_ref):
        z = z_ref[...]
        return ((jnp.tanh(g_ref[...]) + 1.0) * (jnp.tanh(z) + 1.0) * z).astype(F32)

    y_mla = gating(gm_ref, zm_ref) * a_ref[...].astype(F32)
    y_sgu = gating(gs_ref, zs_ref) * u_ref[...].astype(F32) * mix_ref[...]
    twice_merged = (y_mla + y_sgu).astype(BF16)
    y = jnp.dot(twice_merged, wo_bf_ref[...], preferred_element_type=F32)
    o_ref[...] = _rms(x_ref[...] + (0.5 * gate_ref[...]) * y, fg_ref[...])


def _outproj(attn, proj, x2, gate, sg, ws, bs_full, wo, fg):
    tm = 256
    s = x2.shape[0]
    row = lambda i: (i, 0)
    big = lambda k: pl.BlockSpec((tm, D_MODEL), lambda i, k=k: (i, k))
    return pl.pallas_call(
        functools.partial(_out_kernel, tm=tm),
        grid=(s // tm,),
        in_specs=[pl.BlockSpec((tm, D_MODEL), row),
                  big(0), big(1), big(2), big(3), big(4), big(5),
                  pl.BlockSpec((tm, D_MODEL), row), _resident(gate.shape), _resident(sg.shape),
                  _resident(ws.shape), _resident(bs_full.shape), _resident(wo.shape),
                  _resident(fg.shape)],
        out_specs=pl.BlockSpec((tm, D_MODEL), row),
        out_shape=jax.ShapeDtypeStruct((s, D_MODEL), F32),
        scratch_shapes=[pltpu.VMEM((tm, D_MODEL), F32), pltpu.VMEM((D_MODEL, D_MODEL), BF16)],
        compiler_params=pltpu.CompilerParams(
            dimension_semantics=("arbitrary",), vmem_limit_bytes=VMEM_LIMIT),
        name="outproj",
    )(attn, proj, proj, proj, proj, proj, proj, x2, gate, sg, ws, bs_full, wo, fg)


def _rope_tables(positions):
    n_freq = ROPE // 2
    per_row = LANE // n_freq
    inv_freq = 1.0 / (ROPE_THETA ** (jnp.arange(0, ROPE, 2, dtype=F32) / ROPE))
    ang = (positions.astype(F32).reshape(-1, per_row, 1) * inv_freq).reshape(-1, LANE)
    cos = jnp.tile(jnp.cos(ang).reshape(-1, n_freq), (1, per_row))
    sin = jnp.tile(jnp.sin(ang).reshape(-1, n_freq), (1, per_row))
    first = (jnp.arange(LANE) % ROPE) < n_freq
    return cos, jnp.where(first, -sin, 0.0), jnp.where(first, 0.0, sin)


def kernel(x, c, positions, attn_norm_g, w_ada, b_ada, w_in, q_norm_g, w_uq, kv_norm_g, w_ukv,
           sgu_norm_g, w_spatial, b_spatial, w_out, final_norm_g):
    b, s, d = x.shape
    assert (b, s, d) == (1, SEQ, D_MODEL) and attn_norm_g.shape[0] == 1
    x2 = x.reshape(s, d)

    mod = _adaln(jnp.broadcast_to(c, (8, d)), w_ada[0], b_ada[0].reshape(1, 3 * d))[0:1]
    shift, scale, gate = mod[:, :d], mod[:, d:2 * d], mod[:, 2 * d:]

    w_t = jnp.swapaxes(w_in[0], 0, 1)
    wqn, wqr, wk, wvt = _regroup_up_weights(w_uq[0], w_ukv[0])
    h, q, k, vt = _front(x2, attn_norm_g[0].reshape(1, d), scale, shift, w_t,
                         q_norm_g[0].reshape(1, Q_RANK), kv_norm_g[0].reshape(1, KV_RANK),
                         wqn, wqr, wk, wvt, *_rope_tables(positions[0]))

    proj = _inproj(h, w_t)
    attn = _flash(q, k, vt)

    bs_full = jnp.repeat(b_spatial[0].T, CHUNK, axis=1)
    out = _outproj(attn, proj, x2, gate, sgu_norm_g[0].reshape(1, d), w_spatial[0], bs_full,
                   w_out[0], final_norm_g.reshape(1, d))
    return out.reshape(b, s, d)
```

```python
import functools
import math

import jax
import jax.numpy as jnp
from jax import lax
from jax.experimental import pallas as pl
from jax.experimental.pallas import tpu as pltpu

D_MODEL = 2048
SEQ = 8192
HEADS = 16
NOPE = 128
ROPE = 64
VDIM = 128
Q_RANK = 768
KV_RANK = 512
ROPE_THETA = 10000.0
GROUPS = 16
CHUNK = 128
EPS = 1e-6

LANE = 128
HEAD_SLAB = 2 * LANE
KV_BLOCK = 512
N_LAT = Q_RANK + KV_RANK + ROPE
LAT_WIDTH = N_LAT + ROPE
BIG_WIDTH = 6 * D_MODEL
VMEM_LIMIT = 56 * 1024 * 1024

Q_SCALE = (1.0 / math.sqrt(NOPE + ROPE)) * math.log2(math.e)

BF16 = jnp.bfloat16
F32 = jnp.float32
NT_DIMS = (((1,), (1,)), ((), ()))


def _sigmoid(x):
    return 0.5 * jnp.tanh(0.5 * x) + 0.5


def _rms(x, g):
    ms = jnp.mean(x * x, axis=-1, keepdims=True)
    return x * lax.rsqrt(ms + EPS) * g


def _resident(shape):
    return pl.BlockSpec(shape, lambda *_: (0,) * len(shape), pipeline_mode=pl.Buffered(1))


def _adaln_kernel(c_ref, w_ref, b_ref, o_ref):
    c = c_ref[...]
    c_act = c * _sigmoid(c)
    o_ref[...] = jnp.dot(c_act.astype(BF16), w_ref[...].astype(BF16),
                         preferred_element_type=F32) + b_ref[...]


def _adaln(c8, w_ada, b_ada):
    tn = 2048
    n = w_ada.shape[1]
    return pl.pallas_call(
        _adaln_kernel,
        grid=(n // tn,),
        in_specs=[pl.BlockSpec((8, D_MODEL), lambda j: (0, 0)),
                  pl.BlockSpec((D_MODEL, tn), lambda j: (0, j)),
                  pl.BlockSpec((1, tn), lambda j: (0, j))],
        out_specs=pl.BlockSpec((8, tn), lambda j: (0, j)),
        out_shape=jax.ShapeDtypeStruct((8, n), F32),
        compiler_params=pltpu.CompilerParams(
            dimension_semantics=("arbitrary",), vmem_limit_bytes=VMEM_LIMIT),
        name="adaln",
    )(c8, w_ada, b_ada)


def _regroup_kernel(wuq_ref, wukv_ref, wqn_ref, wqr_ref, wk_ref, wvt_ref):
    x0, x1, x2 = (wuq_ref[:, k * LANE:(k + 1) * LANE] for k in range(3))
    first = lax.broadcasted_iota(jnp.int32, x0.shape, 1) < ROPE
    nope_b = jnp.where(first, pltpu.roll(x1, ROPE, 1), pltpu.roll(x2, ROPE, 1))
    wqn_ref[:, :LANE] = x0.astype(BF16)
    wqn_ref[:, LANE:] = nope_b.astype(BF16)
    wqr_ref[...] = jnp.where(first, x1, x2).astype(BF16)
    for hd in range(2):
        lo = hd * (NOPE + VDIM)
        wk_ref[:, hd * NOPE:(hd + 1) * NOPE] = wukv_ref[:, lo:lo + NOPE].astype(BF16)
        wvt_ref[hd * VDIM:(hd + 1) * VDIM, :] = (
            wukv_ref[:, lo + NOPE:lo + NOPE + VDIM].T.astype(BF16))


def _regroup_up_weights(w_uq, w_ukv):
    pairs = HEADS // 2
    col = lambda p: (0, p)
    return pl.pallas_call(
        _regroup_kernel,
        grid=(pairs,),
        in_specs=[pl.BlockSpec((Q_RANK, 2 * (NOPE + ROPE)), col),
                  pl.BlockSpec((KV_RANK, 2 * (NOPE + VDIM)), col)],
        out_specs=[pl.BlockSpec((Q_RANK, 2 * NOPE), col), pl.BlockSpec((Q_RANK, 2 * ROPE), col),
                   pl.BlockSpec((KV_RANK, 2 * NOPE), col),
                   pl.BlockSpec((2 * VDIM, KV_RANK), lambda p: (p, 0))],
        out_shape=[jax.ShapeDtypeStruct((Q_RANK, HEADS * NOPE), BF16),
                   jax.ShapeDtypeStruct((Q_RANK, HEADS * ROPE), BF16),
                   jax.ShapeDtypeStruct((KV_RANK, HEADS * NOPE), BF16),
                   jax.ShapeDtypeStruct((HEADS * VDIM, KV_RANK), BF16)],
        compiler_params=pltpu.CompilerParams(
            dimension_semantics=("parallel",), vmem_limit_bytes=VMEM_LIMIT),
        name="regroup",
    )(w_uq, w_ukv)


def _rope(t, cos4, sin_a, sin_b):
    return t * cos4 + pltpu.roll(t, 96, 1) * sin_a + pltpu.roll(t, 32, 1) * sin_b


def _front_kernel(x_ref, g_ref, scale_ref, shift_ref, wlat_ref, qg_ref, kvg_ref, wqn_ref,
                  wqr_ref, wk_ref, wvt_ref, cos_ref, sa_ref, sb_ref, h_ref, q_ref, k_ref, vt_ref,
                  wlat_bf_ref):
    @pl.when(pl.program_id(0) == 0)
    def _():
        wlat_bf_ref[...] = wlat_ref[...].astype(BF16)

    y = _rms(x_ref[...], g_ref[...])
    h = (y * (1.0 + scale_ref[...]) + shift_ref[...]).astype(BF16)
    h_ref[...] = h
    lat = lax.dot_general(h, wlat_bf_ref[...], NT_DIMS, preferred_element_type=F32)

    cos4, sin_a, sin_b = cos_ref[...], sa_ref[...], sb_ref[...]
    low_half = lax.broadcasted_iota(jnp.int32, cos4.shape, 1) < ROPE

    qn = _rms(lat[:, :Q_RANK], qg_ref[...]).astype(BF16)
    q_nope = jnp.dot(qn, wqn_ref[...], preferred_element_type=F32)
    q_rope = jnp.dot(qn, wqr_ref[...], preferred_element_type=F32)
    for pair in range(HEADS // 2):
        rot = _rope(q_rope[:, pair * LANE:(pair + 1) * LANE], cos4, sin_a, sin_b) * Q_SCALE
        for odd, slab in enumerate((rot, pltpu.roll(rot, ROPE, 1))):
            lo = (2 * pair + odd) * HEAD_SLAB
            q_ref[:, lo + LANE:lo + HEAD_SLAB] = jnp.where(low_half, slab, 0.0).astype(BF16)
    for hd in range(HEADS):
        lo = hd * HEAD_SLAB
        q_ref[:, lo:lo + LANE] = (q_nope[:, hd * LANE:(hd + 1) * LANE] * Q_SCALE).astype(BF16)

    kvn = _rms(lat[:, Q_RANK:Q_RANK + KV_RANK], kvg_ref[...]).astype(BF16)
    vt = lax.dot_general(wvt_ref[...], kvn, NT_DIMS, preferred_element_type=F32)
    vt_ref[...] = vt.astype(BF16).reshape(vt_ref.shape)
    kn = jnp.dot(kvn, wk_ref[...], preferred_element_type=F32)
    kr = _rope(lat[:, Q_RANK + KV_RANK:Q_RANK + KV_RANK + LANE], cos4, sin_a, sin_b)
    kr = jnp.where(low_half, kr, 0.0).astype(BF16)
    for hd in range(HEADS):
        lo = hd * HEAD_SLAB
        k_ref[:, lo:lo + LANE] = kn[:, hd * LANE:(hd + 1) * LANE].astype(BF16)
        k_ref[:, lo + LANE:lo + HEAD_SLAB] = kr


def _front(x2, g, scale, shift, w_t, qg, kvg, wqn, wqr, wk, wvt, cos4, sin_a, sin_b):
    tm = 256
    s = x2.shape[0]
    per_block = KV_BLOCK // tm
    row = lambda i: (i, 0)
    tab = pl.BlockSpec((tm, LANE), row)
    slab = pl.BlockSpec((tm, HEADS * HEAD_SLAB), row)
    vt_spec = pl.BlockSpec((HEADS, 1, VDIM, tm), lambda i: (0, i // per_block, 0, i % per_block))
    wlat_spec = pl.BlockSpec((pl.Element(LAT_WIDTH), pl.Element(D_MODEL)), lambda i: (0, 0),
                             pipeline_mode=pl.Buffered(1))
    return pl.pallas_call(
        _front_kernel,
        grid=(s // tm,),
        in_specs=[pl.BlockSpec((tm, D_MODEL), row),
                  _resident(g.shape), _resident(scale.shape), _resident(shift.shape),
                  wlat_spec, _resident(qg.shape), _resident(kvg.shape),
                  _resident(wqn.shape), _resident(wqr.shape), _resident(wk.shape),
                  _resident(wvt.shape), tab, tab, tab],
        out_specs=[pl.BlockSpec((tm, D_MODEL), row), slab, slab, vt_spec],
        out_shape=[jax.ShapeDtypeStruct((s, D_MODEL), BF16),
                   jax.ShapeDtypeStruct((s, HEADS * HEAD_SLAB), BF16),
                   jax.ShapeDtypeStruct((s, HEADS * HEAD_SLAB), BF16),
                   jax.ShapeDtypeStruct((HEADS, s // KV_BLOCK, VDIM, KV_BLOCK), BF16)],
        scratch_shapes=[pltpu.VMEM((LAT_WIDTH, D_MODEL), BF16)],
        compiler_params=pltpu.CompilerParams(
            dimension_semantics=("arbitrary",), vmem_limit_bytes=VMEM_LIMIT),
        name="front",
    )(x2, g, scale, shift, w_t, qg, kvg, wqn, wqr, wk, wvt, cos4, sin_a, sin_b)


INPROJ_TN = 1536
N_BIG_TILES = BIG_WIDTH // INPROJ_TN


def _inproj_kernel(h_ref, wt_ref, o_ref, wb_ref):
    @pl.when(pl.program_id(1) == 0)
    def _():
        rows = lax.broadcasted_iota(jnp.int32, (INPROJ_TN, 1), 0)
        feat = pl.program_id(0) * INPROJ_TN + rows
        halved = (feat < D_MODEL) | (feat >= 3 * D_MODEL)
        wb_ref[...] = (wt_ref[...] * jnp.where(halved, 0.5, 1.0)).astype(BF16)

    o_ref[...] = lax.dot_general(h_ref[...], wb_ref[...], NT_DIMS,
                                 preferred_element_type=F32).astype(o_ref.dtype)


def _inproj(h, w_t):
    tm, tn = 1024, INPROJ_TN
    s = h.shape[0]
    sub = 8
    w_spec = pl.BlockSpec((pl.Element(tn), pl.Element(D_MODEL)),
                          lambda j, i: ((N_LAT // sub + j * (tn // sub)) * sub, 0))
    return pl.pallas_call(
        _inproj_kernel,
        grid=(N_BIG_TILES, s // tm),
        in_specs=[pl.BlockSpec((tm, D_MODEL), lambda j, i: (i, 0)), w_spec],
        out_specs=pl.BlockSpec((tm, tn), lambda j, i: (i, j)),
        out_shape=jax.ShapeDtypeStruct((s, BIG_WIDTH), BF16),
        scratch_shapes=[pltpu.VMEM((tn, D_MODEL), BF16)],
        compiler_params=pltpu.CompilerParams(
            dimension_semantics=("arbitrary", "arbitrary"), vmem_limit_bytes=VMEM_LIMIT),
        name="inproj",
    )(h, w_t)


ONES_ROWS = 16


def _flash_kernel(q_ref, k_ref, vt_ref, o_ref, s0_ref, s1_ref, m_ref, acc_ref, *, tq, tk):
    i = pl.program_id(1)
    n_diag = tq // tk
    bufs = (s0_ref, s1_ref)
    assert n_diag % 2 == 0

    half = tk // 2

    def qk(k_rows, c_lo, c_hi):
        return lax.dot_general(k_rows, q_ref[c_lo:c_hi, :], NT_DIMS, preferred_element_type=F32)

    def scores_into(s_ref, j, c0=0, causal=False):
        k_blk = k_ref[pl.ds(pl.multiple_of(j * tk, tk), tk), :]
        for c in range(c0, tq, tk):
            if causal and c == c0:
                s_ref[:half, c:c + half] = qk(k_blk[:half], c, c + half)
                s_ref[:, c + half:c + tk] = qk(k_blk, c + half, c + tk)
            else:
                s_ref[:, c:c + tk] = qk(k_blk, c, c + tk)

    def fold(s, v_rows, cols):
        m_prev = m_ref[:, cols]
        m_new = jnp.maximum(m_prev, jnp.max(s, axis=0, keepdims=True))
        alpha = jnp.exp2(m_prev - m_new)
        p = jnp.exp2(s - m_new).astype(BF16)
        pv = jnp.dot(v_rows, p, preferred_element_type=F32)
        acc_ref[:, cols] = alpha * acc_ref[:, cols] + pv
        m_ref[:, cols] = m_new

    def update(s_ref, j, c0=0, causal=False):
        v_ext = jnp.concatenate([vt_ref[0, j], jnp.ones((ONES_ROWS, tk), BF16)], axis=0)
        for c in range(c0, tq, tk):
            if causal and c == c0:
                kv = lax.broadcasted_iota(jnp.int32, (half, half), 0)
                qq = lax.broadcasted_iota(jnp.int32, (half, half), 1)
                tri = kv <= qq
                fold(jnp.where(tri, s_ref[:half, c:c + half], -jnp.inf), v_ext[:, :half],
                     slice(c, c + half))
                late = jnp.where(tri, s_ref[half:, c + half:c + tk], -jnp.inf)
                fold(jnp.concatenate([s_ref[:half, c + half:c + tk], late], axis=0), v_ext,
                     slice(c + half, c + tk))
            else:
                fold(s_ref[:, c:c + tk], v_ext, slice(c, c + tk))

    m_ref[...] = jnp.full(m_ref.shape, -jnp.inf, F32)
    acc_ref[...] = jnp.zeros(acc_ref.shape, F32)
    scores_into(s0_ref, 0)

    def full_blocks(t, carry):
        for u in range(n_diag):
            scores_into(bufs[(u + 1) % 2], n_diag * t + u + 1)
            update(bufs[u % 2], n_diag * t + u)
        return carry

    lax.fori_loop(0, i, full_blocks, 0)

    for u in range(n_diag):
        if u + 1 < n_diag:
            scores_into(bufs[(u + 1) % 2], n_diag * i + u + 1, c0=(u + 1) * tk, causal=True)
        update(bufs[u % 2], n_diag * i + u, c0=u * tk, causal=True)

    o_ref[...] = (acc_ref[:VDIM, :] / acc_ref[VDIM:VDIM + 1, :]).T.astype(o_ref.dtype)


def _flash(q, k, vt):
    tq, tk = 2048, KV_BLOCK
    s = q.shape[0]
    return pl.pallas_call(
        functools.partial(_flash_kernel, tq=tq, tk=tk),
        grid=(HEADS, s // tq),
        in_specs=[pl.BlockSpec((tq, HEAD_SLAB), lambda h, i: (i, h)),
                  pl.BlockSpec((s, HEAD_SLAB), lambda h, i: (0, h)),
                  pl.BlockSpec((1, s // tk, VDIM, tk), lambda h, i: (h, 0, 0, 0))],
        out_specs=pl.BlockSpec((tq, VDIM), lambda h, i: (i, h)),
        out_shape=jax.ShapeDtypeStruct((s, HEADS * VDIM), BF16),
        scratch_shapes=[pltpu.VMEM((tk, tq), F32), pltpu.VMEM((tk, tq), F32),
                        pltpu.VMEM((1, tq), F32), pltpu.VMEM((VDIM + ONES_ROWS, tq), F32)],
        compiler_params=pltpu.CompilerParams(
            dimension_semantics=("parallel", "arbitrary"), vmem_limit_bytes=VMEM_LIMIT),
        name="flash",
    )(q, k, vt)


def _out_kernel(a_ref, zm_ref, u_ref, v_ref, zs_ref, gm_ref, gs_ref, x_ref, gate_ref,
                sg_ref, ws_ref, bs_ref, wo_ref, fg_ref, o_ref, mix_ref, wo_bf_ref, *, tm):
    @pl.when(pl.program_id(0) == 0)
    def _():
        wo_bf_ref[...] = wo_ref[...].astype(BF16)

    rows = lax.broadcasted_iota(jnp.int32, (CHUNK, CHUNK), 0)
    cols = lax.broadcasted_iota(jnp.int32, (CHUNK, CHUNK), 1)
    causal = cols <= rows
    n_chunks = tm // CHUNK

    vn = _rms(v_ref[...].astype(F32), sg_ref[...]).astype(BF16)
    for g in range(GROUPS):
        w = jnp.where(causal, ws_ref[g], 0.0).astype(BF16)
        col = slice(g * CHUNK, (g + 1) * CHUNK)
        rhs = jnp.concatenate([vn[c * CHUNK:(c + 1) * CHUNK, col] for c in range(n_chunks)],
                              axis=1)
        mixed = jnp.dot(w, rhs, preferred_element_type=F32)
        for c in range(n_chunks):
            mix_ref[c * CHUNK:(c + 1) * CHUNK, col] = (mixed[:, c * CHUNK:(c + 1) * CHUNK]
                                                       + bs_ref[:, col])

    def gating(g_ref, z_ref):
        z = z_ref[...]
        return ((jnp.tanh(g_ref[...]) + 1.0) * (jnp.tanh(z) + 1.0) * z).astype(F32)

    y_mla = gating(gm_ref, zm_ref) * a_ref[...].astype(F32)
    y_sgu = gating(gs_ref, zs_ref) * u_ref[...].astype(F32) * mix_ref[...]
    twice_merged = (y_mla + y_sgu).astype(BF16)
    y = jnp.dot(twice_merged, wo_bf_ref[...], preferred_element_type=F32)
    o_ref[...] = _rms(x_ref[...] + (0.5 * gate_ref[...]) * y, fg_ref[...])


def _outproj(attn, proj, x2, gate, sg, ws, bs_full, wo, fg):
    tm = 256
    s = x2.shape[0]
    row = lambda i: (i, 0)
    big = lambda k: pl.BlockSpec((tm, D_MODEL), lambda i, k=k: (i, k))
    return pl.pallas_call(
        functools.partial(_out_kernel, tm=tm),
        grid=(s // tm,),
        in_specs=[pl.BlockSpec((tm, D_MODEL), row),
                  big(0), big(1), big(2), big(3), big(4), big(5),
                  pl.BlockSpec((tm, D_MODEL), row), _resident(gate.shape), _resident(sg.shape),
                  _resident(ws.shape), _resident(bs_full.shape), _resident(wo.shape),
                  _resident(fg.shape)],
        out_specs=pl.BlockSpec((tm, D_MODEL), row),
        out_shape=jax.ShapeDtypeStruct((s, D_MODEL), F32),
        scratch_shapes=[pltpu.VMEM((tm, D_MODEL), F32), pltpu.VMEM((D_MODEL, D_MODEL), BF16)],
        compiler_params=pltpu.CompilerParams(
            dimension_semantics=("arbitrary",), vmem_limit_bytes=VMEM_LIMIT),
        name="outproj",
    )(attn, proj, proj, proj, proj, proj, proj, x2, gate, sg, ws, bs_full, wo, fg)


def _rope_tables(positions):
    n_freq = ROPE // 2
    per_row = LANE // n_freq
    inv_freq = 1.0 / (ROPE_THETA ** (jnp.arange(0, ROPE, 2, dtype=F32) / ROPE))
    ang = (positions.astype(F32).reshape(-1, per_row, 1) * inv_freq).reshape(-1, LANE)
    cos = jnp.tile(jnp.cos(ang).reshape(-1, n_freq), (1, per_row))
    sin = jnp.tile(jnp.sin(ang).reshape(-1, n_freq), (1, per_row))
    first = (jnp.arange(LANE) % ROPE) < n_freq
    return cos, jnp.where(first, -sin, 0.0), jnp.where(first, 0.0, sin)


def kernel(x, c, positions, attn_norm_g, w_ada, b_ada, w_in, q_norm_g, w_uq, kv_norm_g, w_ukv,
           sgu_norm_g, w_spatial, b_spatial, w_out, final_norm_g):
    b, s, d = x.shape
    assert (b, s, d) == (1, SEQ, D_MODEL) and attn_norm_g.shape[0] == 1
    x2 = x.reshape(s, d)

    mod = _adaln(jnp.broadcast_to(c, (8, d)), w_ada[0], b_ada[0].reshape(1, 3 * d))[0:1]
    shift, scale, gate = mod[:, :d], mod[:, d:2 * d], mod[:, 2 * d:]

    w_t = jnp.swapaxes(w_in[0], 0, 1)
    wqn, wqr, wk, wvt = _regroup_up_weights(w_uq[0], w_ukv[0])
    h, q, k, vt = _front(x2, attn_norm_g[0].reshape(1, d), scale, shift, w_t,
                         q_norm_g[0].reshape(1, Q_RANK), kv_norm_g[0].reshape(1, KV_RANK),
                         wqn, wqr, wk, wvt, *_rope_tables(positions[0]))

    proj = _inproj(h, w_t)
    attn = _flash(q, k, vt)

    bs_full = jnp.repeat(b_spatial[0].T, CHUNK, axis=1)
    out = _outproj(attn, proj, x2, gate, sgu_norm_g[0].reshape(1, d), w_spatial[0], bs_full,
                   w_out[0], final_norm_g.reshape(1, d))
    return out.reshape(b, s, d)
```

```python
import functools
import math

import jax
import jax.numpy as jnp
from jax import lax
from jax.experimental import pallas as pl
from jax.experimental.pallas import tpu as pltpu

D_MODEL = 2048
SEQ = 8192
HEADS = 16
NOPE = 128
ROPE = 64
VDIM = 128
Q_RANK = 768
KV_RANK = 512
ROPE_THETA = 10000.0
GROUPS = 16
CHUNK = 128
EPS = 1e-6

LANE = 128
HEAD_SLAB = 2 * LANE
KV_BLOCK = 512
N_LAT = Q_RANK + KV_RANK + ROPE
LAT_WIDTH = N_LAT + ROPE
BIG_WIDTH = 6 * D_MODEL
VMEM_LIMIT = 56 * 1024 * 1024

Q_SCALE = (1.0 / math.sqrt(NOPE + ROPE)) * math.log2(math.e)

BF16 = jnp.bfloat16
F32 = jnp.float32
NT_DIMS = (((1,), (1,)), ((), ()))


def _sigmoid(x):
    return 0.5 * jnp.tanh(0.5 * x) + 0.5


def _rms(x, g):
    ms = jnp.mean(x * x, axis=-1, keepdims=True)
    return x * lax.rsqrt(ms + EPS) * g


def _resident(shape):
    return pl.BlockSpec(shape, lambda *_: (0,) * len(shape), pipeline_mode=pl.Buffered(1))


def _adaln_kernel(c_ref, w_ref, b_ref, o_ref):
    c = c_ref[...]
    c_act = c * _sigmoid(c)
    o_ref[...] = jnp.dot(c_act.astype(BF16), w_ref[...].astype(BF16),
                         preferred_element_type=F32) + b_ref[...]


def _adaln(c8, w_ada, b_ada):
    tn = 2048
    n = w_ada.shape[1]
    return pl.pallas_call(
        _adaln_kernel,
        grid=(n // tn,),
        in_specs=[pl.BlockSpec((8, D_MODEL), lambda j: (0, 0)),
                  pl.BlockSpec((D_MODEL, tn), lambda j: (0, j)),
                  pl.BlockSpec((1, tn), lambda j: (0, j))],
        out_specs=pl.BlockSpec((8, tn), lambda j: (0, j)),
        out_shape=jax.ShapeDtypeStruct((8, n), F32),
        compiler_params=pltpu.CompilerParams(
            dimension_semantics=("arbitrary",), vmem_limit_bytes=VMEM_LIMIT),
        name="adaln",
    )(c8, w_ada, b_ada)


def _regroup_kernel(wuq_ref, wukv_ref, wqn_ref, wqr_ref, wk_ref, wvt_ref):
    x0, x1, x2 = (wuq_ref[:, k * LANE:(k + 1) * LANE] for k in range(3))
    first = lax.broadcasted_iota(jnp.int32, x0.shape, 1) < ROPE
    nope_b = jnp.where(first, pltpu.roll(x1, ROPE, 1), pltpu.roll(x2, ROPE, 1))
    wqn_ref[:, :LANE] = x0.astype(BF16)
    wqn_ref[:, LANE:] = nope_b.astype(BF16)
    wqr_ref[...] = jnp.where(first, x1, x2).astype(BF16)
    for hd in range(2):
        lo = hd * (NOPE + VDIM)
        wk_ref[:, hd * NOPE:(hd + 1) * NOPE] = wukv_ref[:, lo:lo + NOPE].astype(BF16)
        wvt_ref[hd * VDIM:(hd + 1) * VDIM, :] = (
            wukv_ref[:, lo + NOPE:lo + NOPE + VDIM].T.astype(BF16))


def _regroup_up_weights(w_uq, w_ukv):
    pairs = HEADS // 2
    col = lambda p: (0, p)
    return pl.pallas_call(
        _regroup_kernel,
        grid=(pairs,),
        in_specs=[pl.BlockSpec((Q_RANK, 2 * (NOPE + ROPE)), col),
                  pl.BlockSpec((KV_RANK, 2 * (NOPE + VDIM)), col)],
        out_specs=[pl.BlockSpec((Q_RANK, 2 * NOPE), col), pl.BlockSpec((Q_RANK, 2 * ROPE), col),
                   pl.BlockSpec((KV_RANK, 2 * NOPE), col),
                   pl.BlockSpec((2 * VDIM, KV_RANK), lambda p: (p, 0))],
        out_shape=[jax.ShapeDtypeStruct((Q_RANK, HEADS * NOPE), BF16),
                   jax.ShapeDtypeStruct((Q_RANK, HEADS * ROPE), BF16),
                   jax.ShapeDtypeStruct((KV_RANK, HEADS * NOPE), BF16),
                   jax.ShapeDtypeStruct((HEADS * VDIM, KV_RANK), BF16)],
        compiler_params=pltpu.CompilerParams(
            dimension_semantics=("parallel",), vmem_limit_bytes=VMEM_LIMIT),
        name="regroup",
    )(w_uq, w_ukv)


def _rope(t, cos4, sin_a, sin_b):
    return t * cos4 + pltpu.roll(t, 96, 1) * sin_a + pltpu.roll(t, 32, 1) * sin_b


def _front_kernel(x_ref, g_ref, scale_ref, shift_ref, wlat_ref, qg_ref, kvg_ref, wqn_ref,
                  wqr_ref, wk_ref, wvt_ref, cos_ref, sa_ref, sb_ref, h_ref, q_ref, k_ref, vt_ref,
                  wlat_bf_ref):
    @pl.when(pl.program_id(0) == 0)
    def _():
        wlat_bf_ref[...] = wlat_ref[...].astype(BF16)

    y = _rms(x_ref[...], g_ref[...])
    h = (y * (1.0 + scale_ref[...]) + shift_ref[...]).astype(BF16)
    h_ref[...] = h
    lat = lax.dot_general(h, wlat_bf_ref[...], NT_DIMS, preferred_element_type=F32)

    cos4, sin_a, sin_b = cos_ref[...], sa_ref[...], sb_ref[...]
    low_half = lax.broadcasted_iota(jnp.int32, cos4.shape, 1) < ROPE

    qn = _rms(lat[:, :Q_RANK], qg_ref[...]).astype(BF16)
    q_nope = jnp.dot(qn, wqn_ref[...], preferred_element_type=F32)
    q_rope = jnp.dot(qn, wqr_ref[...], preferred_element_type=F32)
    for pair in range(HEADS // 2):
        rot = _rope(q_rope[:, pair * LANE:(pair + 1) * LANE], cos4, sin_a, sin_b) * Q_SCALE
        for odd, slab in enumerate((rot, pltpu.roll(rot, ROPE, 1))):
            lo = (2 * pair + odd) * HEAD_SLAB
            q_ref[:, lo + LANE:lo + HEAD_SLAB] = jnp.where(low_half, slab, 0.0).astype(BF16)
    for hd in range(HEADS):
        lo = hd * HEAD_SLAB
        q_ref[:, lo:lo + LANE] = (q_nope[:, hd * LANE:(hd + 1) * LANE] * Q_SCALE).astype(BF16)

    kvn = _rms(lat[:, Q_RANK:Q_RANK + KV_RANK], kvg_ref[...]).astype(BF16)
    vt = lax.dot_general(wvt_ref[...], kvn, NT_DIMS, preferred_element_type=F32)
    vt_ref[...] = vt.astype(BF16).reshape(vt_ref.shape)
    kn = jnp.dot(kvn, wk_ref[...], preferred_element_type=F32)
    kr = _rope(lat[:, Q_RANK + KV_RANK:Q_RANK + KV_RANK + LANE], cos4, sin_a, sin_b)
    kr = jnp.where(low_half, kr, 0.0).astype(BF16)
    for hd in range(HEADS):
        lo = hd * HEAD_SLAB
        k_ref[:, lo:lo + LANE] = kn[:, hd * LANE:(hd + 1) * LANE].astype(BF16)
        k_ref[:, lo + LANE:lo + HEAD_SLAB] = kr


def _front(x2, g, scale, shift, w_t, qg, kvg, wqn, wqr, wk, wvt, cos4, sin_a, sin_b):
    tm = 256
    s = x2.shape[0]
    per_block = KV_BLOCK // tm
    row = lambda i: (i, 0)
    tab = pl.BlockSpec((tm, LANE), row)
    slab = pl.BlockSpec((tm, HEADS * HEAD_SLAB), row)
    vt_spec = pl.BlockSpec((HEADS, 1, VDIM, tm), lambda i: (0, i // per_block, 0, i % per_block))
    wlat_spec = pl.BlockSpec((pl.Element(LAT_WIDTH), pl.Element(D_MODEL)), lambda i: (0, 0),
                             pipeline_mode=pl.Buffered(1))
    return pl.pallas_call(
        _front_kernel,
        grid=(s // tm,),
        in_specs=[pl.BlockSpec((tm, D_MODEL), row),
                  _resident(g.shape), _resident(scale.shape), _resident(shift.shape),
                  wlat_spec, _resident(qg.shape), _resident(kvg.shape),
                  _resident(wqn.shape), _resident(wqr.shape), _resident(wk.shape),
                  _resident(wvt.shape), tab, tab, tab],
        out_specs=[pl.BlockSpec((tm, D_MODEL), row), slab, slab, vt_spec],
        out_shape=[jax.ShapeDtypeStruct((s, D_MODEL), BF16),
                   jax.ShapeDtypeStruct((s, HEADS * HEAD_SLAB), BF16),
                   jax.ShapeDtypeStruct((s, HEADS * HEAD_SLAB), BF16),
                   jax.ShapeDtypeStruct((HEADS, s // KV_BLOCK, VDIM, KV_BLOCK), BF16)],
        scratch_shapes=[pltpu.VMEM((LAT_WIDTH, D_MODEL), BF16)],
        compiler_params=pltpu.CompilerParams(
            dimension_semantics=("arbitrary",), vmem_limit_bytes=VMEM_LIMIT),
        name="front",
    )(x2, g, scale, shift, w_t, qg, kvg, wqn, wqr, wk, wvt, cos4, sin_a, sin_b)


INPROJ_TN = 1536
N_BIG_TILES = BIG_WIDTH // INPROJ_TN


def _inproj_kernel(h_ref, wt_ref, o_ref, wb_ref):
    @pl.when(pl.program_id(1) == 0)
    def _():
        rows = lax.broadcasted_iota(jnp.int32, (INPROJ_TN, 1), 0)
        feat = pl.program_id(0) * INPROJ_TN + rows
        halved = (feat < D_MODEL) | (feat >= 3 * D_MODEL)
        wb_ref[...] = (wt_ref[...] * jnp.where(halved, 0.5, 1.0)).astype(BF16)

    o_ref[...] = lax.dot_general(h_ref[...], wb_ref[...], NT_DIMS,
                                 preferred_element_type=F32).astype(o_ref.dtype)


def _inproj(h, w_t):
    tm, tn = 1024, INPROJ_TN
    s = h.shape[0]
    sub = 8
    w_spec = pl.BlockSpec((pl.Element(tn), pl.Element(D_MODEL)),
                          lambda j, i: ((N_LAT // sub + j * (tn // sub)) * sub, 0))
    return pl.pallas_call(
        _inproj_kernel,
        grid=(N_BIG_TILES, s // tm),
        in_specs=[pl.BlockSpec((tm, D_MODEL), lambda j, i: (i, 0)), w_spec],
        out_specs=pl.BlockSpec((tm, tn), lambda j, i: (i, j)),
        out_shape=jax.ShapeDtypeStruct((s, BIG_WIDTH), BF16),
        scratch_shapes=[pltpu.VMEM((tn, D_MODEL), BF16)],
        compiler_params=pltpu.CompilerParams(
            dimension_semantics=("arbitrary", "arbitrary"), vmem_limit_bytes=VMEM_LIMIT),
        name="inproj",
    )(h, w_t)


ONES_ROWS = 16


def _flash_kernel(q_ref, k_ref, vt_ref, o_ref, s0_ref, s1_ref, m_ref, acc_ref, *, tq, tk):
    i = pl.program_id(1)
    n_diag = tq // tk
    bufs = (s0_ref, s1_ref)
    assert n_diag % 2 == 0

    half = tk // 2

    def qk(k_rows, c_lo, c_hi):
        return lax.dot_general(k_rows, q_ref[c_lo:c_hi, :], NT_DIMS, preferred_element_type=F32)

    def scores_into(s_ref, j, c0=0, causal=False):
        k_blk = k_ref[pl.ds(pl.multiple_of(j * tk, tk), tk), :]
        for c in range(c0, tq, tk):
            if causal and c == c0:
                s_ref[:half, c:c + half] = qk(k_blk[:half], c, c + half)
                s_ref[:, c + half:c + tk] = qk(k_blk, c + half, c + tk)
            else:
                s_ref[:, c:c + tk] = qk(k_blk, c, c + tk)

    def fold(s, v_rows, cols):
        m_prev = m_ref[:, cols]
        m_new = jnp.maximum(m_prev, jnp.max(s, axis=0, keepdims=True))
        alpha = jnp.exp2(m_prev - m_new)
        p = jnp.exp2(s - m_new).astype(BF16)
        pv = jnp.dot(v_rows, p, preferred_element_type=F32)
        acc_ref[:, cols] = alpha * acc_ref[:, cols] + pv
        m_ref[:, cols] = m_new

    def update(s_ref, j, c0=0, causal=False):
        v_ext = jnp.concatenate([vt_ref[0, j], jnp.ones((ONES_ROWS, tk), BF16)], axis=0)
        for c in range(c0, tq, tk):
            if causal and c == c0:
                kv = lax.broadcasted_iota(jnp.int32, (half, half), 0)
                qq = lax.broadcasted_iota(jnp.int32, (half, half), 1)
                tri = kv <= qq
                fold(jnp.where(tri, s_ref[:half, c:c + half], -jnp.inf), v_ext[:, :half],
                     slice(c, c + half))
                late = jnp.where(tri, s_ref[half:, c + half:c + tk], -jnp.inf)
                fold(jnp.concatenate([s_ref[:half, c + half:c + tk], late], axis=0), v_ext,
                     slice(c + half, c + tk))
            else:
                fold(s_ref[:, c:c + tk], v_ext, slice(c, c + tk))

    m_ref[...] = jnp.full(m_ref.shape, -jnp.inf, F32)
    acc_ref[...] = jnp.zeros(acc_ref.shape, F32)
    scores_into(s0_ref, 0)

    def full_blocks(t, carry):
        for u in range(n_diag):
            scores_into(bufs[(u + 1) % 2], n_diag * t + u + 1)
            update(bufs[u % 2], n_diag * t + u)
        return carry

    lax.fori_loop(0, i, full_blocks, 0)

    for u in range(n_diag):
        if u + 1 < n_diag:
            scores_into(bufs[(u + 1) % 2], n_diag * i + u + 1, c0=(u + 1) * tk, causal=True)
        update(bufs[u % 2], n_diag * i + u, c0=u * tk, causal=True)

    o_ref[...] = (acc_ref[:VDIM, :] / acc_ref[VDIM:VDIM + 1, :]).T.astype(o_ref.dtype)


def _flash(q, k, vt):
    tq, tk = 2048, KV_BLOCK
    s = q.shape[0]
    return pl.pallas_call(
        functools.partial(_flash_kernel, tq=tq, tk=tk),
        grid=(HEADS, s // tq),
        in_specs=[pl.BlockSpec((tq, HEAD_SLAB), lambda h, i: (i, h)),
                  pl.BlockSpec((s, HEAD_SLAB), lambda h, i: (0, h)),
                  pl.BlockSpec((1, s // tk, VDIM, tk), lambda h, i: (h, 0, 0, 0))],
        out_specs=pl.BlockSpec((tq, VDIM), lambda h, i: (i, h)),
        out_shape=jax.ShapeDtypeStruct((s, HEADS * VDIM), BF16),
        scratch_shapes=[pltpu.VMEM((tk, tq), F32), pltpu.VMEM((tk, tq), F32),
                        pltpu.VMEM((1, tq), F32), pltpu.VMEM((VDIM + ONES_ROWS, tq), F32)],
        compiler_params=pltpu.CompilerParams(
            dimension_semantics=("parallel", "arbitrary"), vmem_limit_bytes=VMEM_LIMIT),
        name="flash",
    )(q, k, vt)


def _out_kernel(a_ref, zm_ref, u_ref, v_ref, zs_ref, gm_ref, gs_ref, x_ref, gate_ref,
                sg_ref, ws_ref, bs_ref, wo_ref, fg_ref, o_ref, mix_ref, wo_bf_ref, *, tm):
    @pl.when(pl.program_id(0) == 0)
    def _():
        for c in range(0, D_MODEL, 2 * LANE):
            wo_bf_ref[c:c + 2 * LANE, :] = wo_ref[:, c:c + 2 * LANE].T.astype(BF16)

    rows = lax.broadcasted_iota(jnp.int32, (CHUNK, CHUNK), 0)
    cols = lax.broadcasted_iota(jnp.int32, (CHUNK, CHUNK), 1)
    causal = cols <= rows
    n_chunks = tm // CHUNK

    vn = _rms(v_ref[...].astype(F32), sg_ref[...]).astype(BF16)
    for g in range(GROUPS):
        w = jnp.where(causal, ws_ref[g], 0.0).astype(BF16)
        col = slice(g * CHUNK, (g + 1) * CHUNK)
        rhs = jnp.concatenate([vn[c * CHUNK:(c + 1) * CHUNK, col] for c in range(n_chunks)],
                              axis=1)
        mixed = jnp.dot(w, rhs, preferred_element_type=F32)
        for c in range(n_chunks):
            mix_ref[c * CHUNK:(c + 1) * CHUNK, col] = (mixed[:, c * CHUNK:(c + 1) * CHUNK]
                                                       + bs_ref[:, col])

    def gating(g_ref, z_ref):
        z = z_ref[...]
        return ((jnp.tanh(g_ref[...]) + 1.0) * (jnp.tanh(z) + 1.0) * z).astype(F32)

    y_mla = gating(gm_ref, zm_ref) * a_ref[...].astype(F32)
    y_sgu = gating(gs_ref, zs_ref) * u_ref[...].astype(F32) * mix_ref[...]
    twice_merged = (y_mla + y_sgu).astype(BF16)
    y_t = lax.dot_general(wo_bf_ref[...], twice_merged, NT_DIMS, preferred_element_type=F32)
    o_ref[...] = _rms(x_ref[...] + (0.5 * gate_ref[...]) * y_t.T, fg_ref[...])


def _outproj(attn, proj, x2, gate, sg, ws, bs_full, wo, fg):
    tm = 256
    s = x2.shape[0]
    row = lambda i: (i, 0)
    big = lambda k: pl.BlockSpec((tm, D_MODEL), lambda i, k=k: (i, k))
    return pl.pallas_call(
        functools.partial(_out_kernel, tm=tm),
        grid=(s // tm,),
        in_specs=[pl.BlockSpec((tm, D_MODEL), row),
                  big(0), big(1), big(2), big(3), big(4), big(5),
                  pl.BlockSpec((tm, D_MODEL), row), _resident(gate.shape), _resident(sg.shape),
                  _resident(ws.shape), _resident(bs_full.shape), _resident(wo.shape),
                  _resident(fg.shape)],
        out_specs=pl.BlockSpec((tm, D_MODEL), row),
        out_shape=jax.ShapeDtypeStruct((s, D_MODEL), F32),
        scratch_shapes=[pltpu.VMEM((tm, D_MODEL), F32), pltpu.VMEM((D_MODEL, D_MODEL), BF16)],
        compiler_params=pltpu.CompilerParams(
            dimension_semantics=("arbitrary",), vmem_limit_bytes=VMEM_LIMIT),
        name="outproj",
    )(attn, proj, proj, proj, proj, proj, proj, x2, gate, sg, ws, bs_full, wo, fg)


def _rope_tables(positions):
    n_freq = ROPE // 2
    per_row = LANE // n_freq
    inv_freq = 1.0 / (ROPE_THETA ** (jnp.arange(0, ROPE, 2, dtype=F32) / ROPE))
    ang = (positions.astype(F32).reshape(-1, per_row, 1) * inv_freq).reshape(-1, LANE)
    cos = jnp.tile(jnp.cos(ang).reshape(-1, n_freq), (1, per_row))
    sin = jnp.tile(jnp.sin(ang).reshape(-1, n_freq), (1, per_row))
    first = (jnp.arange(LANE) % ROPE) < n_freq
    return cos, jnp.where(first, -sin, 0.0), jnp.where(first, 0.0, sin)


def kernel(x, c, positions, attn_norm_g, w_ada, b_ada, w_in, q_norm_g, w_uq, kv_norm_g, w_ukv,
           sgu_norm_g, w_spatial, b_spatial, w_out, final_norm_g):
    b, s, d = x.shape
    assert (b, s, d) == (1, SEQ, D_MODEL) and attn_norm_g.shape[0] == 1
    x2 = x.reshape(s, d)

    mod = _adaln(jnp.broadcast_to(c, (8, d)), w_ada[0], b_ada[0].reshape(1, 3 * d))[0:1]
    shift, scale, gate = mod[:, :d], mod[:, d:2 * d], mod[:, 2 * d:]

    w_t = jnp.swapaxes(w_in[0], 0, 1)
    wqn, wqr, wk, wvt = _regroup_up_weights(w_uq[0], w_ukv[0])
    h, q, k, vt = _front(x2, attn_norm_g[0].reshape(1, d), scale, shift, w_t,
                         q_norm_g[0].reshape(1, Q_RANK), kv_norm_g[0].reshape(1, KV_RANK),
                         wqn, wqr, wk, wvt, *_rope_tables(positions[0]))

    proj = _inproj(h, w_t)
    attn = _flash(q, k, vt)

    bs_full = jnp.repeat(b_spatial[0].T, CHUNK, axis=1)
    out = _outproj(attn, proj, x2, gate, sgu_norm_g[0].reshape(1, d), w_spatial[0], bs_full,
                   w_out[0], final_norm_g.reshape(1, d))
    return out.reshape(b, s, d)
```

```python
import functools
import math

import jax
import jax.numpy as jnp
from jax import lax
from jax.experimental import pallas as pl
from jax.experimental.pallas import tpu as pltpu

D_MODEL = 2048
SEQ = 8192
HEADS = 16
NOPE = 128
ROPE = 64
VDIM = 128
Q_RANK = 768
KV_RANK = 512
ROPE_THETA = 10000.0
GROUPS = 16
CHUNK = 128
EPS = 1e-6

LANE = 128
HEAD_SLAB = 2 * LANE
KV_BLOCK = 512
N_LAT = Q_RANK + KV_RANK + ROPE
LAT_WIDTH = N_LAT + ROPE
BIG_WIDTH = 6 * D_MODEL
VMEM_LIMIT = 56 * 1024 * 1024

Q_SCALE = (1.0 / math.sqrt(NOPE + ROPE)) * math.log2(math.e)

BF16 = jnp.bfloat16
F32 = jnp.float32
NT_DIMS = (((1,), (1,)), ((), ()))


def _sigmoid(x):
    return 0.5 * jnp.tanh(0.5 * x) + 0.5


def _rms(x, g):
    ms = jnp.mean(x * x, axis=-1, keepdims=True)
    return x * lax.rsqrt(ms + EPS) * g


def _resident(shape):
    return pl.BlockSpec(shape, lambda *_: (0,) * len(shape), pipeline_mode=pl.Buffered(1))


def _adaln_kernel(c_ref, w_ref, b_ref, o_ref):
    c = c_ref[...]
    c_act = c * _sigmoid(c)
    o_ref[...] = jnp.dot(c_act.astype(BF16), w_ref[...].astype(BF16),
                         preferred_element_type=F32) + b_ref[...]


def _adaln(c8, w_ada, b_ada):
    tn = 2048
    n = w_ada.shape[1]
    return pl.pallas_call(
        _adaln_kernel,
        grid=(n // tn,),
        in_specs=[pl.BlockSpec((8, D_MODEL), lambda j: (0, 0)),
                  pl.BlockSpec((D_MODEL, tn), lambda j: (0, j)),
                  pl.BlockSpec((1, tn), lambda j: (0, j))],
        out_specs=pl.BlockSpec((8, tn), lambda j: (0, j)),
        out_shape=jax.ShapeDtypeStruct((8, n), F32),
        compiler_params=pltpu.CompilerParams(
            dimension_semantics=("arbitrary",), vmem_limit_bytes=VMEM_LIMIT),
        name="adaln",
    )(c8, w_ada, b_ada)


def _regroup_kernel(wuq_ref, wukv_ref, wqn_ref, wqr_ref, wk_ref, wvt_ref):
    x0, x1, x2 = (wuq_ref[:, k * LANE:(k + 1) * LANE] for k in range(3))
    first = lax.broadcasted_iota(jnp.int32, x0.shape, 1) < ROPE
    nope_b = jnp.where(first, pltpu.roll(x1, ROPE, 1), pltpu.roll(x2, ROPE, 1))
    wqn_ref[:, :LANE] = x0.astype(BF16)
    wqn_ref[:, LANE:] = nope_b.astype(BF16)
    wqr_ref[...] = jnp.where(first, x1, x2).astype(BF16)
    for hd in range(2):
        lo = hd * (NOPE + VDIM)
        wk_ref[:, hd * NOPE:(hd + 1) * NOPE] = wukv_ref[:, lo:lo + NOPE].astype(BF16)
        wvt_ref[hd * VDIM:(hd + 1) * VDIM, :] = (
            wukv_ref[:, lo + NOPE:lo + NOPE + VDIM].T.astype(BF16))


def _regroup_up_weights(w_uq, w_ukv):
    pairs = HEADS // 2
    col = lambda p: (0, p)
    return pl.pallas_call(
        _regroup_kernel,
        grid=(pairs,),
        in_specs=[pl.BlockSpec((Q_RANK, 2 * (NOPE + ROPE)), col),
                  pl.BlockSpec((KV_RANK, 2 * (NOPE + VDIM)), col)],
        out_specs=[pl.BlockSpec((Q_RANK, 2 * NOPE), col), pl.BlockSpec((Q_RANK, 2 * ROPE), col),
                   pl.BlockSpec((KV_RANK, 2 * NOPE), col),
                   pl.BlockSpec((2 * VDIM, KV_RANK), lambda p: (p, 0))],
        out_shape=[jax.ShapeDtypeStruct((Q_RANK, HEADS * NOPE), BF16),
                   jax.ShapeDtypeStruct((Q_RANK, HEADS * ROPE), BF16),
                   jax.ShapeDtypeStruct((KV_RANK, HEADS * NOPE), BF16),
                   jax.ShapeDtypeStruct((HEADS * VDIM, KV_RANK), BF16)],
        compiler_params=pltpu.CompilerParams(
            dimension_semantics=("parallel",), vmem_limit_bytes=VMEM_LIMIT),
        name="regroup",
    )(w_uq, w_ukv)


def _rope(t, cos4, sin_a, sin_b):
    return t * cos4 + pltpu.roll(t, 96, 1) * sin_a + pltpu.roll(t, 32, 1) * sin_b


def _front_kernel(x_ref, g_ref, scale_ref, shift_ref, wlat_ref, qg_ref, kvg_ref, wqn_ref,
                  wqr_ref, wk_ref, wvt_ref, cos_ref, sa_ref, sb_ref, h_ref, q_ref, k_ref, vt_ref,
                  wlat_bf_ref):
    @pl.when(pl.program_id(0) == 0)
    def _():
        wlat_bf_ref[...] = wlat_ref[...].astype(BF16)

    y = _rms(x_ref[...], g_ref[...])
    h = (y * (1.0 + scale_ref[...]) + shift_ref[...]).astype(BF16)
    h_ref[...] = h
    lat = lax.dot_general(h, wlat_bf_ref[...], NT_DIMS, preferred_element_type=F32)

    cos4, sin_a, sin_b = cos_ref[...], sa_ref[...], sb_ref[...]
    low_half = lax.broadcasted_iota(jnp.int32, cos4.shape, 1) < ROPE

    qn = _rms(lat[:, :Q_RANK], qg_ref[...]).astype(BF16)
    q_nope = jnp.dot(qn, wqn_ref[...], preferred_element_type=F32)
    q_rope = jnp.dot(qn, wqr_ref[...], preferred_element_type=F32)
    for pair in range(HEADS // 2):
        rot = _rope(q_rope[:, pair * LANE:(pair + 1) * LANE], cos4, sin_a, sin_b) * Q_SCALE
        for odd, slab in enumerate((rot, pltpu.roll(rot, ROPE, 1))):
            lo = (2 * pair + odd) * HEAD_SLAB
            q_ref[:, lo + LANE:lo + HEAD_SLAB] = jnp.where(low_half, slab, 0.0).astype(BF16)
    for hd in range(HEADS):
        lo = hd * HEAD_SLAB
        q_ref[:, lo:lo + LANE] = (q_nope[:, hd * LANE:(hd + 1) * LANE] * Q_SCALE).astype(BF16)

    kvn = _rms(lat[:, Q_RANK:Q_RANK + KV_RANK], kvg_ref[...]).astype(BF16)
    vt = lax.dot_general(wvt_ref[...], kvn, NT_DIMS, preferred_element_type=F32)
    vt_ref[...] = vt.astype(BF16).reshape(vt_ref.shape)
    kn = jnp.dot(kvn, wk_ref[...], preferred_element_type=F32)
    kr = _rope(lat[:, Q_RANK + KV_RANK:Q_RANK + KV_RANK + LANE], cos4, sin_a, sin_b)
    kr = jnp.where(low_half, kr, 0.0).astype(BF16)
    for hd in range(HEADS):
        lo = hd * HEAD_SLAB
        k_ref[:, lo:lo + LANE] = kn[:, hd * LANE:(hd + 1) * LANE].astype(BF16)
        k_ref[:, lo + LANE:lo + HEAD_SLAB] = kr


def _front(x2, g, scale, shift, w_t, qg, kvg, wqn, wqr, wk, wvt, cos4, sin_a, sin_b):
    tm = 256
    s = x2.shape[0]
    per_block = KV_BLOCK // tm
    row = lambda i: (i, 0)
    tab = pl.BlockSpec((tm, LANE), row)
    slab = pl.BlockSpec((tm, HEADS * HEAD_SLAB), row)
    vt_spec = pl.BlockSpec((HEADS, 1, VDIM, tm), lambda i: (0, i // per_block, 0, i % per_block))
    wlat_spec = pl.BlockSpec((pl.Element(LAT_WIDTH), pl.Element(D_MODEL)), lambda i: (0, 0),
                             pipeline_mode=pl.Buffered(1))
    return pl.pallas_call(
        _front_kernel,
        grid=(s // tm,),
        in_specs=[pl.BlockSpec((tm, D_MODEL), row),
                  _resident(g.shape), _resident(scale.shape), _resident(shift.shape),
                  wlat_spec, _resident(qg.shape), _resident(kvg.shape),
                  _resident(wqn.shape), _resident(wqr.shape), _resident(wk.shape),
                  _resident(wvt.shape), tab, tab, tab],
        out_specs=[pl.BlockSpec((tm, D_MODEL), row), slab, slab, vt_spec],
        out_shape=[jax.ShapeDtypeStruct((s, D_MODEL), BF16),
                   jax.ShapeDtypeStruct((s, HEADS * HEAD_SLAB), BF16),
                   jax.ShapeDtypeStruct((s, HEADS * HEAD_SLAB), BF16),
                   jax.ShapeDtypeStruct((HEADS, s // KV_BLOCK, VDIM, KV_BLOCK), BF16)],
        scratch_shapes=[pltpu.VMEM((LAT_WIDTH, D_MODEL), BF16)],
        compiler_params=pltpu.CompilerParams(
            dimension_semantics=("arbitrary",), vmem_limit_bytes=VMEM_LIMIT),
        name="front",
    )(x2, g, scale, shift, w_t, qg, kvg, wqn, wqr, wk, wvt, cos4, sin_a, sin_b)


INPROJ_TN = 1536
N_BIG_TILES = BIG_WIDTH // INPROJ_TN


def _inproj_kernel(h_ref, wt_ref, o_ref, wb_ref):
    @pl.when(pl.program_id(1) == 0)
    def _():
        rows = lax.broadcasted_iota(jnp.int32, (INPROJ_TN, 1), 0)
        feat = pl.program_id(0) * INPROJ_TN + rows
        halved = (feat < D_MODEL) | (feat >= 3 * D_MODEL)
        wb_ref[...] = (wt_ref[...] * jnp.where(halved, 0.5, 1.0)).astype(BF16)

    o_ref[...] = lax.dot_general(h_ref[...], wb_ref[...], NT_DIMS,
                                 preferred_element_type=F32).astype(o_ref.dtype)


def _inproj(h, w_t):
    tm, tn = 1024, INPROJ_TN
    s = h.shape[0]
    sub = 8
    w_spec = pl.BlockSpec((pl.Element(tn), pl.Element(D_MODEL)),
                          lambda j, i: ((N_LAT // sub + j * (tn // sub)) * sub, 0))
    return pl.pallas_call(
        _inproj_kernel,
        grid=(N_BIG_TILES, s // tm),
        in_specs=[pl.BlockSpec((tm, D_MODEL), lambda j, i: (i, 0)), w_spec],
        out_specs=pl.BlockSpec((tm, tn), lambda j, i: (i, j)),
        out_shape=jax.ShapeDtypeStruct((s, BIG_WIDTH), BF16),
        scratch_shapes=[pltpu.VMEM((tn, D_MODEL), BF16)],
        compiler_params=pltpu.CompilerParams(
            dimension_semantics=("arbitrary", "arbitrary"), vmem_limit_bytes=VMEM_LIMIT),
        name="inproj",
    )(h, w_t)


ONES_ROWS = 16


def _flash_kernel(q_ref, k_ref, vt_ref, o_ref, s0_ref, s1_ref, m_ref, acc_ref, *, tq, tk, heads,
                  tiles):
    n_diag = tq // tk
    bufs = (s0_ref, s1_ref)
    assert n_diag % 2 == 0

    half = tk // 2

    def scores_into(s_ref, job, j, c0=0, causal=False):
        tt, hh = job
        lanes = slice(hh * HEAD_SLAB, (hh + 1) * HEAD_SLAB)
        k_blk = k_ref[pl.ds(pl.multiple_of(j * tk, tk), tk), lanes]

        def qk(k_rows, c_lo, c_hi):
            return lax.dot_general(k_rows, q_ref[tt * tq + c_lo:tt * tq + c_hi, lanes], NT_DIMS,
                                   preferred_element_type=F32)

        for c in range(c0, tq, tk):
            if causal and c == c0:
                s_ref[:half, c:c + half] = qk(k_blk[:half], c, c + half)
                s_ref[:, c + half:c + tk] = qk(k_blk, c + half, c + tk)
            else:
                s_ref[:, c:c + tk] = qk(k_blk, c, c + tk)

    def fold(s, v_rows, cols):
        m_prev = m_ref[:, cols]
        m_new = jnp.maximum(m_prev, jnp.max(s, axis=0, keepdims=True))
        alpha = jnp.exp2(m_prev - m_new)
        p = jnp.exp2(s - m_new).astype(BF16)
        pv = jnp.dot(v_rows, p, preferred_element_type=F32)
        acc_ref[:, cols] = alpha * acc_ref[:, cols] + pv
        m_ref[:, cols] = m_new

    def update(s_ref, hh, j, c0=0, causal=False):
        v_ext = jnp.concatenate([vt_ref[hh, j], jnp.ones((ONES_ROWS, tk), BF16)], axis=0)
        for c in range(c0, tq, tk):
            if causal and c == c0:
                kv = lax.broadcasted_iota(jnp.int32, (half, half), 0)
                qq = lax.broadcasted_iota(jnp.int32, (half, half), 1)
                tri = kv <= qq
                fold(jnp.where(tri, s_ref[:half, c:c + half], -jnp.inf), v_ext[:, :half],
                     slice(c, c + half))
                late = jnp.where(tri, s_ref[half:, c + half:c + tk], -jnp.inf)
                fold(jnp.concatenate([s_ref[:half, c + half:c + tk], late], axis=0), v_ext,
                     slice(c + half, c + tk))
            else:
                fold(s_ref[:, c:c + tk], v_ext, slice(c, c + tk))

    jobs = [(tt, hh) for tt in range(tiles) for hh in range(heads)]
    scores_into(s0_ref, jobs[0], 0)
    for n, job in enumerate(jobs):
        tt, hh = job
        i = pl.program_id(1) * tiles + tt
        m_ref[...] = jnp.full(m_ref.shape, -jnp.inf, F32)
        acc_ref[...] = jnp.zeros(acc_ref.shape, F32)

        def full_blocks(t, carry, job=job):
            for u in range(n_diag):
                scores_into(bufs[(u + 1) % 2], job, n_diag * t + u + 1)
                update(bufs[u % 2], job[1], n_diag * t + u)
            return carry

        lax.fori_loop(0, i, full_blocks, 0)

        for u in range(n_diag):
            if u + 1 < n_diag:
                scores_into(bufs[(u + 1) % 2], job, n_diag * i + u + 1, c0=(u + 1) * tk,
                            causal=True)
            elif n + 1 < len(jobs):
                scores_into(bufs[0], jobs[n + 1], 0)
            update(bufs[u % 2], hh, n_diag * i + u, c0=u * tk, causal=True)

        o_ref[tt * tq:(tt + 1) * tq, hh * VDIM:(hh + 1) * VDIM] = (
            acc_ref[:VDIM, :] / acc_ref[VDIM:VDIM + 1, :]).T.astype(o_ref.dtype)


def _flash(q, k, vt):
    tq, tk, heads, tiles = 2048, KV_BLOCK, 2, 2
    s = q.shape[0]
    return pl.pallas_call(
        functools.partial(_flash_kernel, tq=tq, tk=tk, heads=heads, tiles=tiles),
        grid=(HEADS // heads, s // (tq * tiles)),
        in_specs=[pl.BlockSpec((tq * tiles, heads * HEAD_SLAB), lambda h, i: (i, h)),
                  pl.BlockSpec((s, heads * HEAD_SLAB), lambda h, i: (0, h)),
                  pl.BlockSpec((heads, s // tk, VDIM, tk), lambda h, i: (h, 0, 0, 0))],
        out_specs=pl.BlockSpec((tq * tiles, heads * VDIM), lambda h, i: (i, h)),
        out_shape=jax.ShapeDtypeStruct((s, HEADS * VDIM), BF16),
        scratch_shapes=[pltpu.VMEM((tk, tq), F32), pltpu.VMEM((tk, tq), F32),
                        pltpu.VMEM((1, tq), F32), pltpu.VMEM((VDIM + ONES_ROWS, tq), F32)],
        compiler_params=pltpu.CompilerParams(
            dimension_semantics=("parallel", "arbitrary"), vmem_limit_bytes=VMEM_LIMIT),
        name="flash",
    )(q, k, vt)


def _out_kernel(a_ref, zm_ref, u_ref, v_ref, zs_ref, gm_ref, gs_ref, x_ref, gate_ref,
                sg_ref, ws_ref, bs_ref, wo_ref, fg_ref, o_ref, mix_ref, wo_bf_ref, *, tm):
    @pl.when(pl.program_id(0) == 0)
    def _():
        wo_bf_ref[...] = wo_ref[...].astype(BF16)

    rows = lax.broadcasted_iota(jnp.int32, (CHUNK, CHUNK), 0)
    cols = lax.broadcasted_iota(jnp.int32, (CHUNK, CHUNK), 1)
    causal = cols <= rows
    n_chunks = tm // CHUNK

    vn = _rms(v_ref[...].astype(F32), sg_ref[...]).astype(BF16)
    for g in range(GROUPS):
        w = jnp.where(causal, ws_ref[g], 0.0).astype(BF16)
        col = slice(g * CHUNK, (g + 1) * CHUNK)
        rhs = jnp.concatenate([vn[c * CHUNK:(c + 1) * CHUNK, col] for c in range(n_chunks)],
                              axis=1)
        mixed = jnp.dot(w, rhs, preferred_element_type=F32)
        for c in range(n_chunks):
            mix_ref[c * CHUNK:(c + 1) * CHUNK, col] = (mixed[:, c * CHUNK:(c + 1) * CHUNK]
                                                       + bs_ref[:, col])

    def gating(g_ref, z_ref):
        z = z_ref[...]
        return ((jnp.tanh(g_ref[...]) + 1.0) * (jnp.tanh(z) + 1.0) * z).astype(F32)

    y_mla = gating(gm_ref, zm_ref) * a_ref[...].astype(F32)
    y_sgu = gating(gs_ref, zs_ref) * u_ref[...].astype(F32) * mix_ref[...]
    twice_merged = (y_mla + y_sgu).astype(BF16)
    y = jnp.dot(twice_merged, wo_bf_ref[...], preferred_element_type=F32)
    o_ref[...] = _rms(x_ref[...] + (0.5 * gate_ref[...]) * y, fg_ref[...])


def _outproj(attn, proj, x2, gate, sg, ws, bs_full, wo, fg):
    tm = 256
    s = x2.shape[0]
    row = lambda i: (i, 0)
    big = lambda k: pl.BlockSpec((tm, D_MODEL), lambda i, k=k: (i, k))
    return pl.pallas_call(
        functools.partial(_out_kernel, tm=tm),
        grid=(s // tm,),
        in_specs=[pl.BlockSpec((tm, D_MODEL), row),
                  big(0), big(1), big(2), big(3), big(4), big(5),
                  pl.BlockSpec((tm, D_MODEL), row), _resident(gate.shape), _resident(sg.shape),
                  _resident(ws.shape), _resident(bs_full.shape), _resident(wo.shape),
                  _resident(fg.shape)],
        out_specs=pl.BlockSpec((tm, D_MODEL), row),
        out_shape=jax.ShapeDtypeStruct((s, D_MODEL), F32),
        scratch_shapes=[pltpu.VMEM((tm, D_MODEL), F32), pltpu.VMEM((D_MODEL, D_MODEL), BF16)],
        compiler_params=pltpu.CompilerParams(
            dimension_semantics=("arbitrary",), vmem_limit_bytes=VMEM_LIMIT),
        name="outproj",
    )(attn, proj, proj, proj, proj, proj, proj, x2, gate, sg, ws, bs_full, wo, fg)


def _rope_tables(positions):
    n_freq = ROPE // 2
    per_row = LANE // n_freq
    inv_freq = 1.0 / (ROPE_THETA ** (jnp.arange(0, ROPE, 2, dtype=F32) / ROPE))
    ang = (positions.astype(F32).reshape(-1, per_row, 1) * inv_freq).reshape(-1, LANE)
    cos = jnp.tile(jnp.cos(ang).reshape(-1, n_freq), (1, per_row))
    sin = jnp.tile(jnp.sin(ang).reshape(-1, n_freq), (1, per_row))
    first = (jnp.arange(LANE) % ROPE) < n_freq
    return cos, jnp.where(first, -sin, 0.0), jnp.where(first, 0.0, sin)


def kernel(x, c, positions, attn_norm_g, w_ada, b_ada, w_in, q_norm_g, w_uq, kv_norm_g, w_ukv,
           sgu_norm_g, w_spatial, b_spatial, w_out, final_norm_g):
    b, s, d = x.shape
    assert (b, s, d) == (1, SEQ, D_MODEL) and attn_norm_g.shape[0] == 1
    x2 = x.reshape(s, d)

    mod = _adaln(jnp.broadcast_to(c, (8, d)), w_ada[0], b_ada[0].reshape(1, 3 * d))[0:1]
    shift, scale, gate = mod[:, :d], mod[:, d:2 * d], mod[:, 2 * d:]

    w_t = jnp.swapaxes(w_in[0], 0, 1)
    wqn, wqr, wk, wvt = _regroup_up_weights(w_uq[0], w_ukv[0])
    h, q, k, vt = _front(x2, attn_norm_g[0].reshape(1, d), scale, shift, w_t,
                         q_norm_g[0].reshape(1, Q_RANK), kv_norm_g[0].reshape(1, KV_RANK),
                         wqn, wqr, wk, wvt, *_rope_tables(positions[0]))

    proj = _inproj(h, w_t)
    attn = _flash(q, k, vt)

    bs_full = jnp.repeat(b_spatial[0].T, CHUNK, axis=1)
    out = _outproj(attn, proj, x2, gate, sgu_norm_g[0].reshape(1, d), w_spatial[0], bs_full,
                   w_out[0], final_norm_g.reshape(1, d))
    return out.reshape(b, s, d)
```

```python
import functools
import math

import jax
import jax.numpy as jnp
from jax import lax
from jax.experimental import pallas as pl
from jax.experimental.pallas import tpu as pltpu

D_MODEL = 2048
SEQ = 8192
HEADS = 16
NOPE = 128
ROPE = 64
VDIM = 128
Q_RANK = 768
KV_RANK = 512
ROPE_THETA = 10000.0
GROUPS = 16
CHUNK = 128
EPS = 1e-6

LANE = 128
HEAD_SLAB = 2 * LANE
KV_BLOCK = 512
N_LAT = Q_RANK + KV_RANK + ROPE
LAT_WIDTH = N_LAT + ROPE
BIG_WIDTH = 6 * D_MODEL
VMEM_LIMIT = 56 * 1024 * 1024

Q_SCALE = (1.0 / math.sqrt(NOPE + ROPE)) * math.log2(math.e)

BF16 = jnp.bfloat16
F32 = jnp.float32
NT_DIMS = (((1,), (1,)), ((), ()))


def _sigmoid(x):
    return 0.5 * jnp.tanh(0.5 * x) + 0.5


def _rms(x, g):
    ms = jnp.mean(x * x, axis=-1, keepdims=True)
    return x * lax.rsqrt(ms + EPS) * g


def _resident(shape):
    return pl.BlockSpec(shape, lambda *_: (0,) * len(shape), pipeline_mode=pl.Buffered(1))


def _adaln_kernel(c_ref, w_ref, b_ref, o_ref):
    c = c_ref[...]
    c_act = c * _sigmoid(c)
    for lo in range(0, o_ref.shape[1], LANE):
        o_ref[:, lo:lo + LANE] = (jnp.sum(w_ref[:, lo:lo + LANE] * c_act, axis=0, keepdims=True)
                                  + b_ref[:, lo:lo + LANE])


def _adaln(c_col, w_ada, b_ada):
    tn = 2048
    n = w_ada.shape[1]
    return pl.pallas_call(
        _adaln_kernel,
        grid=(n // tn,),
        in_specs=[pl.BlockSpec((D_MODEL, LANE), lambda j: (0, 0)),
                  pl.BlockSpec((D_MODEL, tn), lambda j: (0, j)),
                  pl.BlockSpec((1, tn), lambda j: (0, j))],
        out_specs=pl.BlockSpec((1, tn), lambda j: (0, j)),
        out_shape=jax.ShapeDtypeStruct((1, n), F32),
        compiler_params=pltpu.CompilerParams(
            dimension_semantics=("arbitrary",), vmem_limit_bytes=VMEM_LIMIT),
        name="adaln",
    )(c_col, w_ada, b_ada)


def _regroup_kernel(wuq_ref, wukv_ref, wqn_ref, wqr_ref, wk_ref, wvt_ref):
    x0, x1, x2 = (wuq_ref[:, k * LANE:(k + 1) * LANE] for k in range(3))
    first = lax.broadcasted_iota(jnp.int32, x0.shape, 1) < ROPE
    nope_b = jnp.where(first, pltpu.roll(x1, ROPE, 1), pltpu.roll(x2, ROPE, 1))
    wqn_ref[:, :LANE] = x0.astype(BF16)
    wqn_ref[:, LANE:] = nope_b.astype(BF16)
    wqr_ref[...] = jnp.where(first, x1, x2).astype(BF16)
    for hd in range(2):
        lo = hd * (NOPE + VDIM)
        wk_ref[:, hd * NOPE:(hd + 1) * NOPE] = wukv_ref[:, lo:lo + NOPE].astype(BF16)
        wvt_ref[hd * VDIM:(hd + 1) * VDIM, :] = (
            wukv_ref[:, lo + NOPE:lo + NOPE + VDIM].T.astype(BF16))


def _regroup_up_weights(w_uq, w_ukv):
    pairs = HEADS // 2
    col = lambda p: (0, p)
    return pl.pallas_call(
        _regroup_kernel,
        grid=(pairs,),
        in_specs=[pl.BlockSpec((Q_RANK, 2 * (NOPE + ROPE)), col),
                  pl.BlockSpec((KV_RANK, 2 * (NOPE + VDIM)), col)],
        out_specs=[pl.BlockSpec((Q_RANK, 2 * NOPE), col), pl.BlockSpec((Q_RANK, 2 * ROPE), col),
                   pl.BlockSpec((KV_RANK, 2 * NOPE), col),
                   pl.BlockSpec((2 * VDIM, KV_RANK), lambda p: (p, 0))],
        out_shape=[jax.ShapeDtypeStruct((Q_RANK, HEADS * NOPE), BF16),
                   jax.ShapeDtypeStruct((Q_RANK, HEADS * ROPE), BF16),
                   jax.ShapeDtypeStruct((KV_RANK, HEADS * NOPE), BF16),
                   jax.ShapeDtypeStruct((HEADS * VDIM, KV_RANK), BF16)],
        compiler_params=pltpu.CompilerParams(
            dimension_semantics=("parallel",), vmem_limit_bytes=VMEM_LIMIT),
        name="regroup",
    )(w_uq, w_ukv)


def _rope(t, cos4, sin_a, sin_b):
    return t * cos4 + pltpu.roll(t, 96, 1) * sin_a + pltpu.roll(t, 32, 1) * sin_b


def _front_kernel(x_ref, g_ref, scale_ref, shift_ref, wlat_ref, qg_ref, kvg_ref, wqn_ref,
                  wqr_ref, wk_ref, wvt_ref, cos_ref, sa_ref, sb_ref, h_ref, q_ref, k_ref, vt_ref,
                  wlat_bf_ref):
    @pl.when(pl.program_id(0) == 0)
    def _():
        wlat_bf_ref[...] = wlat_ref[...].astype(BF16)

    y = _rms(x_ref[...], g_ref[...])
    h = (y * (1.0 + scale_ref[...]) + shift_ref[...]).astype(BF16)
    h_ref[...] = h
    lat = lax.dot_general(h, wlat_bf_ref[...], NT_DIMS, preferred_element_type=F32)

    cos4, sin_a, sin_b = cos_ref[...], sa_ref[...], sb_ref[...]
    low_half = lax.broadcasted_iota(jnp.int32, cos4.shape, 1) < ROPE

    qn = _rms(lat[:, :Q_RANK], qg_ref[...]).astype(BF16)
    q_nope = jnp.dot(qn, wqn_ref[...], preferred_element_type=F32)
    q_rope = jnp.dot(qn, wqr_ref[...], preferred_element_type=F32)
    for pair in range(HEADS // 2):
        rot = _rope(q_rope[:, pair * LANE:(pair + 1) * LANE], cos4, sin_a, sin_b) * Q_SCALE
        for odd, slab in enumerate((rot, pltpu.roll(rot, ROPE, 1))):
            lo = (2 * pair + odd) * HEAD_SLAB
            q_ref[:, lo + LANE:lo + HEAD_SLAB] = jnp.where(low_half, slab, 0.0).astype(BF16)
    for hd in range(HEADS):
        lo = hd * HEAD_SLAB
        q_ref[:, lo:lo + LANE] = (q_nope[:, hd * LANE:(hd + 1) * LANE] * Q_SCALE).astype(BF16)

    kvn = _rms(lat[:, Q_RANK:Q_RANK + KV_RANK], kvg_ref[...]).astype(BF16)
    vt = lax.dot_general(wvt_ref[...], kvn, NT_DIMS, preferred_element_type=F32)
    vt_ref[...] = vt.astype(BF16).reshape(vt_ref.shape)
    kn = jnp.dot(kvn, wk_ref[...], preferred_element_type=F32)
    kr = _rope(lat[:, Q_RANK + KV_RANK:Q_RANK + KV_RANK + LANE], cos4, sin_a, sin_b)
    kr = jnp.where(low_half, kr, 0.0).astype(BF16)
    for hd in range(HEADS):
        lo = hd * HEAD_SLAB
        k_ref[:, lo:lo + LANE] = kn[:, hd * LANE:(hd + 1) * LANE].astype(BF16)
        k_ref[:, lo + LANE:lo + HEAD_SLAB] = kr


def _front(x2, g, scale, shift, w_t, qg, kvg, wqn, wqr, wk, wvt, cos4, sin_a, sin_b):
    tm = 256
    s = x2.shape[0]
    per_block = KV_BLOCK // tm
    row = lambda i: (i, 0)
    tab = pl.BlockSpec((tm, LANE), row)
    slab = pl.BlockSpec((tm, HEADS * HEAD_SLAB), row)
    vt_spec = pl.BlockSpec((HEADS, 1, VDIM, tm), lambda i: (0, i // per_block, 0, i % per_block))
    wlat_spec = pl.BlockSpec((pl.Element(LAT_WIDTH), pl.Element(D_MODEL)), lambda i: (0, 0),
                             pipeline_mode=pl.Buffered(1))
    return pl.pallas_call(
        _front_kernel,
        grid=(s // tm,),
        in_specs=[pl.BlockSpec((tm, D_MODEL), row),
                  _resident(g.shape), _resident(scale.shape), _resident(shift.shape),
                  wlat_spec, _resident(qg.shape), _resident(kvg.shape),
                  _resident(wqn.shape), _resident(wqr.shape), _resident(wk.shape),
                  _resident(wvt.shape), tab, tab, tab],
        out_specs=[pl.BlockSpec((tm, D_MODEL), row), slab, slab, vt_spec],
        out_shape=[jax.ShapeDtypeStruct((s, D_MODEL), BF16),
                   jax.ShapeDtypeStruct((s, HEADS * HEAD_SLAB), BF16),
                   jax.ShapeDtypeStruct((s, HEADS * HEAD_SLAB), BF16),
                   jax.ShapeDtypeStruct((HEADS, s // KV_BLOCK, VDIM, KV_BLOCK), BF16)],
        scratch_shapes=[pltpu.VMEM((LAT_WIDTH, D_MODEL), BF16)],
        compiler_params=pltpu.CompilerParams(
            dimension_semantics=("arbitrary",), vmem_limit_bytes=VMEM_LIMIT),
        name="front",
    )(x2, g, scale, shift, w_t, qg, kvg, wqn, wqr, wk, wvt, cos4, sin_a, sin_b)


INPROJ_TN = 1024
N_BIG_TILES = BIG_WIDTH // INPROJ_TN


def _inproj_kernel(h_ref, wt_ref, o_ref, wb_ref):
    @pl.when(pl.program_id(1) == 0)
    def _():
        rows = lax.broadcasted_iota(jnp.int32, (INPROJ_TN, 1), 0)
        feat = pl.program_id(0) * INPROJ_TN + rows
        halved = (feat < D_MODEL) | (feat >= 3 * D_MODEL)
        wb_ref[...] = (wt_ref[...] * jnp.where(halved, 0.5, 1.0)).astype(BF16)

    o_ref[...] = lax.dot_general(h_ref[...], wb_ref[...], NT_DIMS,
                                 preferred_element_type=F32).astype(o_ref.dtype)


def _inproj(h, w_t):
    tm, tn = 2048, INPROJ_TN
    s = h.shape[0]
    sub = 8
    w_spec = pl.BlockSpec((pl.Element(tn), pl.Element(D_MODEL)),
                          lambda j, i: ((N_LAT // sub + j * (tn // sub)) * sub, 0))
    return pl.pallas_call(
        _inproj_kernel,
        grid=(N_BIG_TILES, s // tm),
        in_specs=[pl.BlockSpec((tm, D_MODEL), lambda j, i: (i, 0)), w_spec],
        out_specs=pl.BlockSpec((tm, tn), lambda j, i: (i, j)),
        out_shape=jax.ShapeDtypeStruct((s, BIG_WIDTH), BF16),
        scratch_shapes=[pltpu.VMEM((tn, D_MODEL), BF16)],
        compiler_params=pltpu.CompilerParams(
            dimension_semantics=("arbitrary", "arbitrary"), vmem_limit_bytes=VMEM_LIMIT),
        name="inproj",
    )(h, w_t)


ONES_ROWS = 16


def _flash_kernel(q_ref, k_ref, vt_ref, o_ref, s0_ref, s1_ref, m_ref, acc_ref, *, tq, tk, heads):
    i = pl.program_id(1)
    n_diag = tq // tk
    bufs = (s0_ref, s1_ref)
    assert n_diag % 2 == 0

    half = tk // 2

    def scores_into(s_ref, hh, j, c0=0, causal=False):
        lanes = slice(hh * HEAD_SLAB, (hh + 1) * HEAD_SLAB)
        k_blk = k_ref[pl.ds(pl.multiple_of(j * tk, tk), tk), lanes]

        def qk(k_rows, c_lo, c_hi):
            return lax.dot_general(k_rows, q_ref[c_lo:c_hi, lanes], NT_DIMS,
                                   preferred_element_type=F32)

        for c in range(c0, tq, tk):
            if causal and c == c0:
                s_ref[:half, c:c + half] = qk(k_blk[:half], c, c + half)
                s_ref[:, c + half:c + tk] = qk(k_blk, c + half, c + tk)
            else:
                s_ref[:, c:c + tk] = qk(k_blk, c, c + tk)

    def fold(s, v_rows, cols):
        m_prev = m_ref[:, cols]
        m_new = jnp.maximum(m_prev, jnp.max(s, axis=0, keepdims=True))
        alpha = jnp.exp2(m_prev - m_new)
        p = jnp.exp2(s - m_new).astype(BF16)
        pv = jnp.dot(v_rows, p, preferred_element_type=F32)
        acc_ref[:, cols] = alpha * acc_ref[:, cols] + pv
        m_ref[:, cols] = m_new

    def update(s_ref, hh, j, c0=0, causal=False):
        v_ext = jnp.concatenate([vt_ref[hh, j], jnp.ones((ONES_ROWS, tk), BF16)], axis=0)
        for c in range(c0, tq, tk):
            if causal and c == c0:
                kv = lax.broadcasted_iota(jnp.int32, (half, half), 0)
                qq = lax.broadcasted_iota(jnp.int32, (half, half), 1)
                tri = kv <= qq
                fold(jnp.where(tri, s_ref[:half, c:c + half], -jnp.inf), v_ext[:, :half],
                     slice(c, c + half))
                late = jnp.where(tri, s_ref[half:, c + half:c + tk], -jnp.inf)
                fold(jnp.concatenate([s_ref[:half, c + half:c + tk], late], axis=0), v_ext,
                     slice(c + half, c + tk))
            else:
                fold(s_ref[:, c:c + tk], v_ext, slice(c, c + tk))

    scores_into(s0_ref, 0, 0)
    for hh in range(heads):
        m_ref[...] = jnp.full(m_ref.shape, -jnp.inf, F32)
        acc_ref[...] = jnp.zeros(acc_ref.shape, F32)

        def full_blocks(t, carry, hh=hh):
            for u in range(n_diag):
                scores_into(bufs[(u + 1) % 2], hh, n_diag * t + u + 1)
                update(bufs[u % 2], hh, n_diag * t + u)
            return carry

        lax.fori_loop(0, i, full_blocks, 0)

        for u in range(n_diag):
            if u + 1 < n_diag:
                scores_into(bufs[(u + 1) % 2], hh, n_diag * i + u + 1, c0=(u + 1) * tk,
                            causal=True)
            elif hh + 1 < heads:
                scores_into(bufs[0], hh + 1, 0)
            update(bufs[u % 2], hh, n_diag * i + u, c0=u * tk, causal=True)

        o_ref[:, hh * VDIM:(hh + 1) * VDIM] = (
            acc_ref[:VDIM, :] / acc_ref[VDIM:VDIM + 1, :]).T.astype(o_ref.dtype)


def _flash(q, k, vt):
    tq, tk, heads = 2048, KV_BLOCK, 2
    s = q.shape[0]
    return pl.pallas_call(
        functools.partial(_flash_kernel, tq=tq, tk=tk, heads=heads),
        grid=(HEADS // heads, s // tq),
        in_specs=[pl.BlockSpec((tq, heads * HEAD_SLAB), lambda h, i: (i, h)),
                  pl.BlockSpec((s, heads * HEAD_SLAB), lambda h, i: (0, h)),
                  pl.BlockSpec((heads, s // tk, VDIM, tk), lambda h, i: (h, 0, 0, 0))],
        out_specs=pl.BlockSpec((tq, heads * VDIM), lambda h, i: (i, h)),
        out_shape=jax.ShapeDtypeStruct((s, HEADS * VDIM), BF16),
        scratch_shapes=[pltpu.VMEM((tk, tq), F32), pltpu.VMEM((tk, tq), F32),
                        pltpu.VMEM((1, tq), F32), pltpu.VMEM((VDIM + ONES_ROWS, tq), F32)],
        compiler_params=pltpu.CompilerParams(
            dimension_semantics=("parallel", "arbitrary"), vmem_limit_bytes=VMEM_LIMIT),
        name="flash",
    )(q, k, vt)


def _out_kernel(a_ref, zm_ref, u_ref, v_ref, zs_ref, gm_ref, gs_ref, x_ref, gate_ref,
                sg_ref, ws_ref, bs_ref, wo_ref, fg_ref, o_ref, mix_ref, wo_bf_ref, *, tm):
    @pl.when(pl.program_id(0) == 0)
    def _():
        wo_bf_ref[...] = wo_ref[...].astype(BF16)

    rows = lax.broadcasted_iota(jnp.int32, (CHUNK, CHUNK), 0)
    cols = lax.broadcasted_iota(jnp.int32, (CHUNK, CHUNK), 1)
    causal = cols <= rows
    n_chunks = tm // CHUNK

    vn = _rms(v_ref[...].astype(F32), sg_ref[...]).astype(BF16)
    for g in range(GROUPS):
        w = jnp.where(causal, ws_ref[g], 0.0).astype(BF16)
        col = slice(g * CHUNK, (g + 1) * CHUNK)
        rhs = jnp.concatenate([vn[c * CHUNK:(c + 1) * CHUNK, col] for c in range(n_chunks)],
                              axis=1)
        mixed = jnp.dot(w, rhs, preferred_element_type=F32)
        for c in range(n_chunks):
            mix_ref[c * CHUNK:(c + 1) * CHUNK, col] = (mixed[:, c * CHUNK:(c + 1) * CHUNK]
                                                       + bs_ref[:, col])

    def gating(g_ref, z_ref):
        z = z_ref[...]
        return ((jnp.tanh(g_ref[...]) + 1.0) * (jnp.tanh(z) + 1.0) * z).astype(F32)

    y_mla = gating(gm_ref, zm_ref) * a_ref[...].astype(F32)
    y_sgu = gating(gs_ref, zs_ref) * u_ref[...].astype(F32) * mix_ref[...]
    twice_merged = (y_mla + y_sgu).astype(BF16)
    y = jnp.dot(twice_merged, wo_bf_ref[...], preferred_element_type=F32)
    o_ref[...] = _rms(x_ref[...] + (0.5 * gate_ref[...]) * y, fg_ref[...])


def _outproj(attn, proj, x2, gate, sg, ws, bs_full, wo, fg):
    tm = 256
    s = x2.shape[0]
    row = lambda i: (i, 0)
    big = lambda k: pl.BlockSpec((tm, D_MODEL), lambda i, k=k: (i, k))
    return pl.pallas_call(
        functools.partial(_out_kernel, tm=tm),
        grid=(s // tm,),
        in_specs=[pl.BlockSpec((tm, D_MODEL), row),
                  big(0), big(1), big(2), big(3), big(4), big(5),
                  pl.BlockSpec((tm, D_MODEL), row), _resident(gate.shape), _resident(sg.shape),
                  _resident(ws.shape), _resident(bs_full.shape), _resident(wo.shape),
                  _resident(fg.shape)],
        out_specs=pl.BlockSpec((tm, D_MODEL), row),
        out_shape=jax.ShapeDtypeStruct((s, D_MODEL), F32),
        scratch_shapes=[pltpu.VMEM((tm, D_MODEL), F32), pltpu.VMEM((D_MODEL, D_MODEL), BF16)],
        compiler_params=pltpu.CompilerParams(
            dimension_semantics=("arbitrary",), vmem_limit_bytes=VMEM_LIMIT),
        name="outproj",
    )(attn, proj, proj, proj, proj, proj, proj, x2, gate, sg, ws, bs_full, wo, fg)


def _rope_tables(positions):
    n_freq = ROPE // 2
    per_row = LANE // n_freq
    inv_freq = 1.0 / (ROPE_THETA ** (jnp.arange(0, ROPE, 2, dtype=F32) / ROPE))
    ang = (positions.astype(F32).reshape(-1, per_row, 1) * inv_freq).reshape(-1, LANE)
    cos = jnp.tile(jnp.cos(ang).reshape(-1, n_freq), (1, per_row))
    sin = jnp.tile(jnp.sin(ang).reshape(-1, n_freq), (1, per_row))
    first = (jnp.arange(LANE) % ROPE) < n_freq
    return cos, jnp.where(first, -sin, 0.0), jnp.where(first, 0.0, sin)


def kernel(x, c, positions, attn_norm_g, w_ada, b_ada, w_in, q_norm_g, w_uq, kv_norm_g, w_ukv,
           sgu_norm_g, w_spatial, b_spatial, w_out, final_norm_g):
    b, s, d = x.shape
    assert (b, s, d) == (1, SEQ, D_MODEL) and attn_norm_g.shape[0] == 1
    x2 = x.reshape(s, d)

    mod = _adaln(jnp.broadcast_to(c.reshape(d, 1), (d, LANE)), w_ada[0],
                 b_ada[0].reshape(1, 3 * d))
    shift, scale, gate = mod[:, :d], mod[:, d:2 * d], mod[:, 2 * d:]

    w_t = jnp.swapaxes(w_in[0], 0, 1)
    wqn, wqr, wk, wvt = _regroup_up_weights(w_uq[0], w_ukv[0])
    h, q, k, vt = _front(x2, attn_norm_g[0].reshape(1, d), scale, shift, w_t,
                         q_norm_g[0].reshape(1, Q_RANK), kv_norm_g[0].reshape(1, KV_RANK),
                         wqn, wqr, wk, wvt, *_rope_tables(positions[0]))

    proj = _inproj(h, w_t)
    attn = _flash(q, k, vt)

    bs_full = jnp.repeat(b_spatial[0].T, CHUNK, axis=1)
    out = _outproj(attn, proj, x2, gate, sgu_norm_g[0].reshape(1, d), w_spatial[0], bs_full,
                   w_out[0], final_norm_g.reshape(1, d))
    return out.reshape(b, s, d)
```

```python
import functools
import math

import jax
import jax.numpy as jnp
from jax import lax
from jax.experimental import pallas as pl
from jax.experimental.pallas import tpu as pltpu

D_MODEL = 2048
SEQ = 8192
HEADS = 16
NOPE = 128
ROPE = 64
VDIM = 128
Q_RANK = 768
KV_RANK = 512
ROPE_THETA = 10000.0
GROUPS = 16
CHUNK = 128
EPS = 1e-6

LANE = 128
HEAD_SLAB = 2 * LANE
KV_BLOCK = 512
N_LAT = Q_RANK + KV_RANK + ROPE
LAT_WIDTH = N_LAT + ROPE
BIG_WIDTH = 6 * D_MODEL
VMEM_LIMIT = 56 * 1024 * 1024

Q_SCALE = (1.0 / math.sqrt(NOPE + ROPE)) * math.log2(math.e)

BF16 = jnp.bfloat16
F32 = jnp.float32
NT_DIMS = (((1,), (1,)), ((), ()))


def _sigmoid(x):
    return 0.5 * jnp.tanh(0.5 * x) + 0.5


def _rms(x, g):
    ms = jnp.mean(x * x, axis=-1, keepdims=True)
    return x * lax.rsqrt(ms + EPS) * g


def _resident(shape):
    return pl.BlockSpec(shape, lambda *_: (0,) * len(shape), pipeline_mode=pl.Buffered(1))


def _adaln_kernel(c_ref, w_ref, b_ref, o_ref):
    c = c_ref[...]
    c_act = c * _sigmoid(c)
    o_ref[...] = jnp.dot(c_act.astype(BF16), w_ref[...].astype(BF16),
                         preferred_element_type=F32) + b_ref[...]


def _adaln(c8, w_ada, b_ada):
    tn = 2048
    n = w_ada.shape[1]
    return pl.pallas_call(
        _adaln_kernel,
        grid=(n // tn,),
        in_specs=[pl.BlockSpec((8, D_MODEL), lambda j: (0, 0)),
                  pl.BlockSpec((D_MODEL, tn), lambda j: (0, j)),
                  pl.BlockSpec((1, tn), lambda j: (0, j))],
        out_specs=pl.BlockSpec((8, tn), lambda j: (0, j)),
        out_shape=jax.ShapeDtypeStruct((8, n), F32),
        compiler_params=pltpu.CompilerParams(
            dimension_semantics=("arbitrary",), vmem_limit_bytes=VMEM_LIMIT),
        name="adaln",
    )(c8, w_ada, b_ada)


def _regroup_kernel(wuq_ref, wukv_ref, wqn_ref, wqr_ref, wk_ref, wvt_ref):
    x0, x1, x2 = (wuq_ref[:, k * LANE:(k + 1) * LANE] for k in range(3))
    first = lax.broadcasted_iota(jnp.int32, x0.shape, 1) < ROPE
    nope_b = jnp.where(first, pltpu.roll(x1, ROPE, 1), pltpu.roll(x2, ROPE, 1))
    wqn_ref[:, :LANE] = x0.astype(BF16)
    wqn_ref[:, LANE:] = nope_b.astype(BF16)
    wqr_ref[...] = jnp.where(first, x1, x2).astype(BF16)
    for hd in range(2):
        lo = hd * (NOPE + VDIM)
        wk_ref[:, hd * NOPE:(hd + 1) * NOPE] = wukv_ref[:, lo:lo + NOPE].astype(BF16)
        wvt_ref[hd * VDIM:(hd + 1) * VDIM, :] = (
            wukv_ref[:, lo + NOPE:lo + NOPE + VDIM].T.astype(BF16))


def _regroup_up_weights(w_uq, w_ukv):
    pairs = HEADS // 2
    col = lambda p: (0, p)
    return pl.pallas_call(
        _regroup_kernel,
        grid=(pairs,),
        in_specs=[pl.BlockSpec((Q_RANK, 2 * (NOPE + ROPE)), col),
                  pl.BlockSpec((KV_RANK, 2 * (NOPE + VDIM)), col)],
        out_specs=[pl.BlockSpec((Q_RANK, 2 * NOPE), col), pl.BlockSpec((Q_RANK, 2 * ROPE), col),
                   pl.BlockSpec((KV_RANK, 2 * NOPE), col),
                   pl.BlockSpec((2 * VDIM, KV_RANK), lambda p: (p, 0))],
        out_shape=[jax.ShapeDtypeStruct((Q_RANK, HEADS * NOPE), BF16),
                   jax.ShapeDtypeStruct((Q_RANK, HEADS * ROPE), BF16),
                   jax.ShapeDtypeStruct((KV_RANK, HEADS * NOPE), BF16),
                   jax.ShapeDtypeStruct((HEADS * VDIM, KV_RANK), BF16)],
        compiler_params=pltpu.CompilerParams(
            dimension_semantics=("parallel",), vmem_limit_bytes=VMEM_LIMIT),
        name="regroup",
    )(w_uq, w_ukv)


def _rope(t, cos4, sin_a, sin_b):
    return t * cos4 + pltpu.roll(t, 96, 1) * sin_a + pltpu.roll(t, 32, 1) * sin_b


def _front_kernel(x_ref, g_ref, scale_ref, shift_ref, wlat_ref, qg_ref, kvg_ref, wqn_ref,
                  wqr_ref, wk_ref, wvt_ref, cos_ref, sa_ref, sb_ref, h_ref, q_ref, k_ref, vt_ref,
                  wlat_bf_ref):
    @pl.when(pl.program_id(0) == 0)
    def _():
        wlat_bf_ref[...] = wlat_ref[...].astype(BF16)

    y = _rms(x_ref[...], g_ref[...])
    h = (y * (1.0 + scale_ref[...]) + shift_ref[...]).astype(BF16)
    h_ref[...] = h
    lat = lax.dot_general(h, wlat_bf_ref[...], NT_DIMS, preferred_element_type=F32)

    cos4, sin_a, sin_b = cos_ref[...], sa_ref[...], sb_ref[...]
    low_half = lax.broadcasted_iota(jnp.int32, cos4.shape, 1) < ROPE

    qn = _rms(lat[:, :Q_RANK], qg_ref[...]).astype(BF16)
    q_nope = jnp.dot(qn, wqn_ref[...], preferred_element_type=F32)
    q_rope = jnp.dot(qn, wqr_ref[...], preferred_element_type=F32)
    for pair in range(HEADS // 2):
        rot = _rope(q_rope[:, pair * LANE:(pair + 1) * LANE], cos4, sin_a, sin_b) * Q_SCALE
        for odd, slab in enumerate((rot, pltpu.roll(rot, ROPE, 1))):
            lo = (2 * pair + odd) * HEAD_SLAB
            q_ref[:, lo + LANE:lo + HEAD_SLAB] = jnp.where(low_half, slab, 0.0).astype(BF16)
    for hd in range(HEADS):
        lo = hd * HEAD_SLAB
        q_ref[:, lo:lo + LANE] = (q_nope[:, hd * LANE:(hd + 1) * LANE] * Q_SCALE).astype(BF16)

    kvn = _rms(lat[:, Q_RANK:Q_RANK + KV_RANK], kvg_ref[...]).astype(BF16)
    vt = lax.dot_general(wvt_ref[...], kvn, NT_DIMS, preferred_element_type=F32)
    vt_ref[...] = vt.astype(BF16).reshape(vt_ref.shape)
    kn = jnp.dot(kvn, wk_ref[...], preferred_element_type=F32)
    kr = _rope(lat[:, Q_RANK + KV_RANK:Q_RANK + KV_RANK + LANE], cos4, sin_a, sin_b)
    kr = jnp.where(low_half, kr, 0.0).astype(BF16)
    for hd in range(HEADS):
        lo = hd * HEAD_SLAB
        k_ref[:, lo:lo + LANE] = kn[:, hd * LANE:(hd + 1) * LANE].astype(BF16)
        k_ref[:, lo + LANE:lo + HEAD_SLAB] = kr


def _front(x2, g, scale, shift, w_t, qg, kvg, wqn, wqr, wk, wvt, cos4, sin_a, sin_b):
    tm = 256
    s = x2.shape[0]
    per_block = KV_BLOCK // tm
    row = lambda i: (i, 0)
    tab = pl.BlockSpec((tm, LANE), row)
    slab = pl.BlockSpec((tm, HEADS * HEAD_SLAB), row)
    vt_spec = pl.BlockSpec((HEADS, 1, VDIM, tm), lambda i: (0, i // per_block, 0, i % per_block))
    wlat_spec = pl.BlockSpec((pl.Element(LAT_WIDTH), pl.Element(D_MODEL)), lambda i: (0, 0),
                             pipeline_mode=pl.Buffered(1))
    return pl.pallas_call(
        _front_kernel,
        grid=(s // tm,),
        in_specs=[pl.BlockSpec((tm, D_MODEL), row),
                  _resident(g.shape), _resident(scale.shape), _resident(shift.shape),
                  wlat_spec, _resident(qg.shape), _resident(kvg.shape),
                  _resident(wqn.shape), _resident(wqr.shape), _resident(wk.shape),
                  _resident(wvt.shape), tab, tab, tab],
        out_specs=[pl.BlockSpec((tm, D_MODEL), row), slab, slab, vt_spec],
        out_shape=[jax.ShapeDtypeStruct((s, D_MODEL), BF16),
                   jax.ShapeDtypeStruct((s, HEADS * HEAD_SLAB), BF16),
                   jax.ShapeDtypeStruct((s, HEADS * HEAD_SLAB), BF16),
                   jax.ShapeDtypeStruct((HEADS, s // KV_BLOCK, VDIM, KV_BLOCK), BF16)],
        scratch_shapes=[pltpu.VMEM((LAT_WIDTH, D_MODEL), BF16)],
        compiler_params=pltpu.CompilerParams(
            dimension_semantics=("arbitrary",), vmem_limit_bytes=VMEM_LIMIT),
        name="front",
    )(x2, g, scale, shift, w_t, qg, kvg, wqn, wqr, wk, wvt, cos4, sin_a, sin_b)


INPROJ_TN = 1536
N_BIG_TILES = BIG_WIDTH // INPROJ_TN


def _inproj_kernel(h_ref, wt_ref, o_ref, wb_ref):
    @pl.when(pl.program_id(1) == 0)
    def _():
        rows = lax.broadcasted_iota(jnp.int32, (INPROJ_TN, 1), 0)
        feat = pl.program_id(0) * INPROJ_TN + rows
        halved = (feat < D_MODEL) | (feat >= 3 * D_MODEL)
        wb_ref[...] = (wt_ref[...] * jnp.where(halved, 0.5, 1.0)).astype(BF16)

    o_ref[...] = lax.dot_general(h_ref[...], wb_ref[...], NT_DIMS,
                                 preferred_element_type=F32).astype(o_ref.dtype)


def _inproj(h, w_t):
    tm, tn = 1024, INPROJ_TN
    s = h.shape[0]
    sub = 8
    w_spec = pl.BlockSpec((pl.Element(tn), pl.Element(D_MODEL)),
                          lambda j, i: ((N_LAT // sub + j * (tn // sub)) * sub, 0))
    return pl.pallas_call(
        _inproj_kernel,
        grid=(N_BIG_TILES, s // tm),
        in_specs=[pl.BlockSpec((tm, D_MODEL), lambda j, i: (i, 0)), w_spec],
        out_specs=pl.BlockSpec((tm, tn), lambda j, i: (i, j)),
        out_shape=jax.ShapeDtypeStruct((s, BIG_WIDTH), BF16),
        scratch_shapes=[pltpu.VMEM((tn, D_MODEL), BF16)],
        compiler_params=pltpu.CompilerParams(
            dimension_semantics=("arbitrary", "arbitrary"), vmem_limit_bytes=VMEM_LIMIT),
        name="inproj",
    )(h, w_t)


ONES_ROWS = 16


def _flash_kernel(q_ref, k_ref, vt_ref, o_ref, s0_ref, s1_ref, m_ref, acc_ref, *, tq, tk, heads):
    i = pl.program_id(1)
    n_diag = tq // tk
    bufs = (s0_ref, s1_ref)
    assert n_diag % 2 == 0

    half = tk // 2

    def scores_into(s_ref, hh, j, c0=0, causal=False):
        lanes = slice(hh * HEAD_SLAB, (hh + 1) * HEAD_SLAB)
        k_blk = k_ref[pl.ds(pl.multiple_of(j * tk, tk), tk), lanes]

        def qk(k_rows, c_lo, c_hi):
            return lax.dot_general(k_rows, q_ref[c_lo:c_hi, lanes], NT_DIMS,
                                   preferred_element_type=F32)

        for c in range(c0, tq, tk):
            if causal and c == c0:
                s_ref[:half, c:c + half] = qk(k_blk[:half], c, c + half)
                s_ref[:, c + half:c + tk] = qk(k_blk, c + half, c + tk)
            else:
                s_ref[:, c:c + tk] = qk(k_blk, c, c + tk)

    def fold(s, v_rows, cols):
        m_prev = m_ref[:, cols]
        m_new = jnp.maximum(m_prev, jnp.max(s, axis=0, keepdims=True))
        alpha = jnp.exp2(m_prev - m_new)
        p = jnp.exp2(s - m_new).astype(BF16)
        pv = jnp.dot(v_rows, p, preferred_element_type=F32)
        acc_ref[:, cols] = alpha * acc_ref[:, cols] + pv
        m_ref[:, cols] = m_new

    def update(s_ref, hh, j, c0=0, causal=False):
        v_ext = jnp.concatenate([vt_ref[hh, j], jnp.ones((ONES_ROWS, tk), BF16)], axis=0)
        for c in range(c0, tq, tk):
            if causal and c == c0:
                kv = lax.broadcasted_iota(jnp.int32, (half, half), 0)
                qq = lax.broadcasted_iota(jnp.int32, (half, half), 1)
                tri = kv <= qq
                fold(jnp.where(tri, s_ref[:half, c:c + half], -jnp.inf), v_ext[:, :half],
                     slice(c, c + half))
                late = jnp.where(tri, s_ref[half:, c + half:c + tk], -jnp.inf)
                fold(jnp.concatenate([s_ref[:half, c + half:c + tk], late], axis=0), v_ext,
                     slice(c + half, c + tk))
            else:
                fold(s_ref[:, c:c + tk], v_ext, slice(c, c + tk))

    scores_into(s0_ref, 0, 0)
    for hh in range(heads):
        m_ref[...] = jnp.full(m_ref.shape, -jnp.inf, F32)
        acc_ref[...] = jnp.zeros(acc_ref.shape, F32)

        def full_blocks(t, carry, hh=hh):
            for u in range(n_diag):
                scores_into(bufs[(u + 1) % 2], hh, n_diag * t + u + 1)
                update(bufs[u % 2], hh, n_diag * t + u)
            return carry

        lax.fori_loop(0, i, full_blocks, 0)

        for u in range(n_diag):
            if u + 1 < n_diag:
                scores_into(bufs[(u + 1) % 2], hh, n_diag * i + u + 1, c0=(u + 1) * tk,
                            causal=True)
            elif hh + 1 < heads:
                scores_into(bufs[0], hh + 1, 0)
            update(bufs[u % 2], hh, n_diag * i + u, c0=u * tk, causal=True)

        o_ref[:, hh * VDIM:(hh + 1) * VDIM] = (
            acc_ref[:VDIM, :] / acc_ref[VDIM:VDIM + 1, :]).T.astype(o_ref.dtype)


def _flash(q, k, vt):
    tq, tk, heads = 2048, KV_BLOCK, 2
    s = q.shape[0]
    return pl.pallas_call(
        functools.partial(_flash_kernel, tq=tq, tk=tk, heads=heads),
        grid=(HEADS // heads, s // tq),
        in_specs=[pl.BlockSpec((tq, heads * HEAD_SLAB), lambda h, i: (i, h)),
                  pl.BlockSpec((s, heads * HEAD_SLAB), lambda h, i: (0, h)),
                  pl.BlockSpec((heads, s // tk, VDIM, tk), lambda h, i: (h, 0, 0, 0))],
        out_specs=pl.BlockSpec((tq, heads * VDIM), lambda h, i: (i, h)),
        out_shape=jax.ShapeDtypeStruct((s, HEADS * VDIM), BF16),
        scratch_shapes=[pltpu.VMEM((tk, tq), F32), pltpu.VMEM((tk, tq), F32),
                        pltpu.VMEM((1, tq), F32), pltpu.VMEM((VDIM + ONES_ROWS, tq), F32)],
        compiler_params=pltpu.CompilerParams(
            dimension_semantics=("parallel", "arbitrary"), vmem_limit_bytes=VMEM_LIMIT),
        name="flash",
    )(q, k, vt)


def _out_kernel(a_ref, zm_ref, u_ref, v_ref, zs_ref, gm_ref, gs_ref, x_ref, gate_ref,
                sg_ref, ws_ref, bs_ref, wo_ref, fg_ref, o_ref, mix_ref, wo_bf_ref, *, tm):
    @pl.when(pl.program_id(0) == 0)
    def _():
        wo_bf_ref[...] = wo_ref[...].astype(BF16)

    rows = lax.broadcasted_iota(jnp.int32, (CHUNK, CHUNK), 0)
    cols = lax.broadcasted_iota(jnp.int32, (CHUNK, CHUNK), 1)
    causal = cols <= rows
    n_chunks = tm // CHUNK

    vn = _rms(v_ref[...].astype(F32), sg_ref[...]).astype(BF16)
    for g in range(GROUPS):
        w = jnp.where(causal, ws_ref[g], 0.0).astype(BF16)
        col = slice(g * CHUNK, (g + 1) * CHUNK)
        rhs = jnp.concatenate([vn[c * CHUNK:(c + 1) * CHUNK, col] for c in range(n_chunks)],
                              axis=1)
        mixed = jnp.dot(w, rhs, preferred_element_type=F32)
        for c in range(n_chunks):
            mix_ref[c * CHUNK:(c + 1) * CHUNK, col] = (mixed[:, c * CHUNK:(c + 1) * CHUNK]
                                                       + bs_ref[:, col])

    def gating(g_ref, z_ref):
        z = z_ref[...]
        return ((jnp.tanh(g_ref[...]) + 1.0) * (jnp.tanh(z) + 1.0) * z).astype(F32)

    y_mla = gating(gm_ref, zm_ref) * a_ref[...].astype(F32)
    y_sgu = gating(gs_ref, zs_ref) * u_ref[...].astype(F32) * mix_ref[...]
    twice_merged = (y_mla + y_sgu).astype(BF16)
    y = jnp.dot(twice_merged, wo_bf_ref[...], preferred_element_type=F32)
    o_ref[...] = _rms(x_ref[...] + (0.5 * gate_ref[...]) * y, fg_ref[...])


def _outproj(attn, proj, x2, gate, sg, ws, bs_full, wo, fg):
    tm = 256
    s = x2.shape[0]
    row = lambda i: (i, 0)
    big = lambda k: pl.BlockSpec((tm, D_MODEL), lambda i, k=k: (i, k))
    return pl.pallas_call(
        functools.partial(_out_kernel, tm=tm),
        grid=(s // tm,),
        in_specs=[pl.BlockSpec((tm, D_MODEL), row),
                  big(0), big(1), big(2), big(3), big(4), big(5),
                  pl.BlockSpec((tm, D_MODEL), row), _resident(gate.shape), _resident(sg.shape),
                  _resident(ws.shape), _resident(bs_full.shape), _resident(wo.shape),
                  _resident(fg.shape)],
        out_specs=pl.BlockSpec((tm, D_MODEL), row),
        out_shape=jax.ShapeDtypeStruct((s, D_MODEL), F32),
        scratch_shapes=[pltpu.VMEM((tm, D_MODEL), F32), pltpu.VMEM((D_MODEL, D_MODEL), BF16)],
        compiler_params=pltpu.CompilerParams(
            dimension_semantics=("arbitrary",), vmem_limit_bytes=VMEM_LIMIT),
        name="outproj",
    )(attn, proj, proj, proj, proj, proj, proj, x2, gate, sg, ws, bs_full, wo, fg)


def _rope_tables(positions):
    n_freq = ROPE // 2
    per_row = LANE // n_freq
    inv_freq = 1.0 / (ROPE_THETA ** (jnp.arange(0, ROPE, 2, dtype=F32) / ROPE))
    ang = (positions.astype(F32).reshape(-1, per_row, 1) * inv_freq).reshape(-1, LANE)
    cos = jnp.tile(jnp.cos(ang).reshape(-1, n_freq), (1, per_row))
    sin = jnp.tile(jnp.sin(ang).reshape(-1, n_freq), (1, per_row))
    first = (jnp.arange(LANE) % ROPE) < n_freq
    return cos, jnp.where(first, -sin, 0.0), jnp.where(first, 0.0, sin)


def kernel(x, c, positions, attn_norm_g, w_ada, b_ada, w_in, q_norm_g, w_uq, kv_norm_g, w_ukv,
           sgu_norm_g, w_spatial, b_spatial, w_out, final_norm_g):
    b, s, d = x.shape
    assert (b, s, d) == (1, SEQ, D_MODEL) and attn_norm_g.shape[0] == 1
    x2 = x.reshape(s, d)

    mod = _adaln(jnp.broadcast_to(c, (8, d)), w_ada[0], b_ada[0].reshape(1, 3 * d))[0:1]
    shift, scale, gate = mod[:, :d], mod[:, d:2 * d], mod[:, 2 * d:]

    w_t = jnp.swapaxes(w_in[0], 0, 1)
    wqn, wqr, wk, wvt = _regroup_up_weights(w_uq[0], w_ukv[0])
    h, q, k, vt = _front(x2, attn_norm_g[0].reshape(1, d), scale, shift, w_t,
                         q_norm_g[0].reshape(1, Q_RANK), kv_norm_g[0].reshape(1, KV_RANK),
                         wqn, wqr, wk, wvt, *_rope_tables(positions[0]))

    proj = _inproj(h, w_t)
    attn = _flash(q, k, vt)

    bs_full = jnp.repeat(b_spatial[0].T, CHUNK, axis=1)
    out = _outproj(attn, proj, x2, gate, sgu_norm_g[0].reshape(1, d), w_spatial[0], bs_full,
                   w_out[0], final_norm_g.reshape(1, d))
    return out.reshape(b, s, d)
```

```python
import functools
import math

import jax
import jax.numpy as jnp
from jax import lax
from jax.experimental import pallas as pl
from jax.experimental.pallas import tpu as pltpu

D_MODEL = 2048
SEQ = 8192
HEADS = 16
NOPE = 128
ROPE = 64
VDIM = 128
Q_RANK = 768
KV_RANK = 512
ROPE_THETA = 10000.0
GROUPS = 16
CHUNK = 128
EPS = 1e-6

LANE = 128
HEAD_SLAB = 2 * LANE
KV_BLOCK = 512
N_LAT = Q_RANK + KV_RANK + ROPE
LAT_WIDTH = N_LAT + ROPE
BIG_WIDTH = 6 * D_MODEL
VMEM_LIMIT = 56 * 1024 * 1024

Q_SCALE = (1.0 / math.sqrt(NOPE + ROPE)) * math.log2(math.e)

BF16 = jnp.bfloat16
F32 = jnp.float32
NT_DIMS = (((1,), (1,)), ((), ()))


def _sigmoid(x):
    return 0.5 * jnp.tanh(0.5 * x) + 0.5


def _rms(x, g):
    ms = jnp.mean(x * x, axis=-1, keepdims=True)
    return x * lax.rsqrt(ms + EPS) * g


def _resident(shape):
    return pl.BlockSpec(shape, lambda *_: (0,) * len(shape), pipeline_mode=pl.Buffered(1))


def _adaln_kernel(c_ref, w_ref, b_ref, o_ref):
    c = c_ref[...]
    c_act = c * _sigmoid(c)
    o_ref[...] = jnp.dot(c_act.astype(BF16), w_ref[...].astype(BF16),
                         preferred_element_type=F32) + b_ref[...]


def _adaln(c8, w_ada, b_ada):
    tn = 2048
    n = w_ada.shape[1]
    return pl.pallas_call(
        _adaln_kernel,
        grid=(n // tn,),
        in_specs=[pl.BlockSpec((8, D_MODEL), lambda j: (0, 0)),
                  pl.BlockSpec((D_MODEL, tn), lambda j: (0, j)),
                  pl.BlockSpec((1, tn), lambda j: (0, j))],
        out_specs=pl.BlockSpec((8, tn), lambda j: (0, j)),
        out_shape=jax.ShapeDtypeStruct((8, n), F32),
        compiler_params=pltpu.CompilerParams(
            dimension_semantics=("arbitrary",), vmem_limit_bytes=VMEM_LIMIT),
        name="adaln",
    )(c8, w_ada, b_ada)


def _regroup_kernel(wuq_ref, wukv_ref, wqn_ref, wqr_ref, wk_ref, wvt_ref):
    x0, x1, x2 = (wuq_ref[:, k * LANE:(k + 1) * LANE] for k in range(3))
    first = lax.broadcasted_iota(jnp.int32, x0.shape, 1) < ROPE
    nope_b = jnp.where(first, pltpu.roll(x1, ROPE, 1), pltpu.roll(x2, ROPE, 1))
    wqn_ref[:, :LANE] = x0.astype(BF16)
    wqn_ref[:, LANE:] = nope_b.astype(BF16)
    wqr_ref[...] = jnp.where(first, x1, x2).astype(BF16)
    for hd in range(2):
        lo = hd * (NOPE + VDIM)
        wk_ref[:, hd * NOPE:(hd + 1) * NOPE] = wukv_ref[:, lo:lo + NOPE].astype(BF16)
        wvt_ref[hd * VDIM:(hd + 1) * VDIM, :] = (
            wukv_ref[:, lo + NOPE:lo + NOPE + VDIM].T.astype(BF16))


def _regroup_up_weights(w_uq, w_ukv):
    pairs = HEADS // 2
    col = lambda p: (0, p)
    return pl.pallas_call(
        _regroup_kernel,
        grid=(pairs,),
        in_specs=[pl.BlockSpec((Q_RANK, 2 * (NOPE + ROPE)), col),
                  pl.BlockSpec((KV_RANK, 2 * (NOPE + VDIM)), col)],
        out_specs=[pl.BlockSpec((Q_RANK, 2 * NOPE), col), pl.BlockSpec((Q_RANK, 2 * ROPE), col),
                   pl.BlockSpec((KV_RANK, 2 * NOPE), col),
                   pl.BlockSpec((2 * VDIM, KV_RANK), lambda p: (p, 0))],
        out_shape=[jax.ShapeDtypeStruct((Q_RANK, HEADS * NOPE), BF16),
                   jax.ShapeDtypeStruct((Q_RANK, HEADS * ROPE), BF16),
                   jax.ShapeDtypeStruct((KV_RANK, HEADS * NOPE), BF16),
                   jax.ShapeDtypeStruct((HEADS * VDIM, KV_RANK), BF16)],
        compiler_params=pltpu.CompilerParams(
            dimension_semantics=("parallel",), vmem_limit_bytes=VMEM_LIMIT),
        name="regroup",
    )(w_uq, w_ukv)


def _rope(t, cos4, sin_a, sin_b):
    return t * cos4 + pltpu.roll(t, 96, 1) * sin_a + pltpu.roll(t, 32, 1) * sin_b


def _front_kernel(x_ref, g_ref, scale_ref, shift_ref, wlat_ref, qg_ref, kvg_ref, wqn_ref,
                  wqr_ref, wk_ref, wvt_ref, rope_ref, h_ref, q_ref, k_ref, vt_ref,
                  wlat_bf_ref):
    @pl.when(pl.program_id(0) == 0)
    def _():
        wlat_bf_ref[...] = wlat_ref[...].astype(BF16)

    y = _rms(x_ref[...], g_ref[...])
    h = (y * (1.0 + scale_ref[...]) + shift_ref[...]).astype(BF16)
    h_ref[...] = h
    lat = lax.dot_general(h, wlat_bf_ref[...], NT_DIMS, preferred_element_type=F32)

    cos4, sin_a, sin_b = (rope_ref[:, k * LANE:(k + 1) * LANE] for k in range(3))
    low_half = lax.broadcasted_iota(jnp.int32, cos4.shape, 1) < ROPE

    qn = _rms(lat[:, :Q_RANK], qg_ref[...]).astype(BF16)
    q_nope = jnp.dot(qn, wqn_ref[...], preferred_element_type=F32)
    q_rope = jnp.dot(qn, wqr_ref[...], preferred_element_type=F32)
    for pair in range(HEADS // 2):
        rot = _rope(q_rope[:, pair * LANE:(pair + 1) * LANE], cos4, sin_a, sin_b) * Q_SCALE
        for odd, slab in enumerate((rot, pltpu.roll(rot, ROPE, 1))):
            lo = (2 * pair + odd) * HEAD_SLAB
            q_ref[:, lo + LANE:lo + HEAD_SLAB] = jnp.where(low_half, slab, 0.0).astype(BF16)
    for hd in range(HEADS):
        lo = hd * HEAD_SLAB
        q_ref[:, lo:lo + LANE] = (q_nope[:, hd * LANE:(hd + 1) * LANE] * Q_SCALE).astype(BF16)

    kvn = _rms(lat[:, Q_RANK:Q_RANK + KV_RANK], kvg_ref[...]).astype(BF16)
    vt = lax.dot_general(wvt_ref[...], kvn, NT_DIMS, preferred_element_type=F32)
    vt_ref[...] = vt.astype(BF16).reshape(vt_ref.shape)
    kn = jnp.dot(kvn, wk_ref[...], preferred_element_type=F32)
    kr = _rope(lat[:, Q_RANK + KV_RANK:Q_RANK + KV_RANK + LANE], cos4, sin_a, sin_b)
    kr = jnp.where(low_half, kr, 0.0).astype(BF16)
    for hd in range(HEADS):
        lo = hd * HEAD_SLAB
        k_ref[:, lo:lo + LANE] = kn[:, hd * LANE:(hd + 1) * LANE].astype(BF16)
        k_ref[:, lo + LANE:lo + HEAD_SLAB] = kr


def _front(x2, g, scale, shift, w_t, qg, kvg, wqn, wqr, wk, wvt, rope):
    tm = 256
    s = x2.shape[0]
    per_block = KV_BLOCK // tm
    row = lambda i: (i, 0)
    tab = pl.BlockSpec((tm, 3 * LANE), row)
    slab = pl.BlockSpec((tm, HEADS * HEAD_SLAB), row)
    vt_spec = pl.BlockSpec((HEADS, 1, VDIM, tm), lambda i: (0, i // per_block, 0, i % per_block))
    wlat_spec = pl.BlockSpec((pl.Element(LAT_WIDTH), pl.Element(D_MODEL)), lambda i: (0, 0),
                             pipeline_mode=pl.Buffered(1))
    return pl.pallas_call(
        _front_kernel,
        grid=(s // tm,),
        in_specs=[pl.BlockSpec((tm, D_MODEL), row),
                  _resident(g.shape), _resident(scale.shape), _resident(shift.shape),
                  wlat_spec, _resident(qg.shape), _resident(kvg.shape),
                  _resident(wqn.shape), _resident(wqr.shape), _resident(wk.shape),
                  _resident(wvt.shape), tab],
        out_specs=[pl.BlockSpec((tm, D_MODEL), row), slab, slab, vt_spec],
        out_shape=[jax.ShapeDtypeStruct((s, D_MODEL), BF16),
                   jax.ShapeDtypeStruct((s, HEADS * HEAD_SLAB), BF16),
                   jax.ShapeDtypeStruct((s, HEADS * HEAD_SLAB), BF16),
                   jax.ShapeDtypeStruct((HEADS, s // KV_BLOCK, VDIM, KV_BLOCK), BF16)],
        scratch_shapes=[pltpu.VMEM((LAT_WIDTH, D_MODEL), BF16)],
        compiler_params=pltpu.CompilerParams(
            dimension_semantics=("arbitrary",), vmem_limit_bytes=VMEM_LIMIT),
        name="front",
    )(x2, g, scale, shift, w_t, qg, kvg, wqn, wqr, wk, wvt, rope)


INPROJ_TN = 1536
N_BIG_TILES = BIG_WIDTH // INPROJ_TN


def _inproj_kernel(h_ref, wt_ref, o_ref, wb_ref):
    @pl.when(pl.program_id(1) == 0)
    def _():
        rows = lax.broadcasted_iota(jnp.int32, (INPROJ_TN, 1), 0)
        feat = pl.program_id(0) * INPROJ_TN + rows
        halved = (feat < D_MODEL) | (feat >= 3 * D_MODEL)
        wb_ref[...] = (wt_ref[...] * jnp.where(halved, 0.5, 1.0)).astype(BF16)

    o_ref[...] = lax.dot_general(h_ref[...], wb_ref[...], NT_DIMS,
                                 preferred_element_type=F32).astype(o_ref.dtype)


def _inproj(h, w_t):
    tm, tn = 1024, INPROJ_TN
    s = h.shape[0]
    sub = 8
    w_spec = pl.BlockSpec((pl.Element(tn), pl.Element(D_MODEL)),
                          lambda j, i: ((N_LAT // sub + j * (tn // sub)) * sub, 0))
    return pl.pallas_call(
        _inproj_kernel,
        grid=(N_BIG_TILES, s // tm),
        in_specs=[pl.BlockSpec((tm, D_MODEL), lambda j, i: (i, 0)), w_spec],
        out_specs=pl.BlockSpec((tm, tn), lambda j, i: (i, j)),
        out_shape=jax.ShapeDtypeStruct((s, BIG_WIDTH), BF16),
        scratch_shapes=[pltpu.VMEM((tn, D_MODEL), BF16)],
        compiler_params=pltpu.CompilerParams(
            dimension_semantics=("arbitrary", "arbitrary"), vmem_limit_bytes=VMEM_LIMIT),
        name="inproj",
    )(h, w_t)


ONES_ROWS = 16


def _flash_kernel(q_ref, k_ref, vt_ref, o_ref, s0_ref, s1_ref, m_ref, acc_ref, *, tq, tk, heads):
    i = pl.program_id(1)
    n_diag = tq // tk
    bufs = (s0_ref, s1_ref)
    assert n_diag % 2 == 0

    half = tk // 2

    def scores_into(s_ref, hh, j, c0=0, causal=False):
        lanes = slice(hh * HEAD_SLAB, (hh + 1) * HEAD_SLAB)
        k_blk = k_ref[pl.ds(pl.multiple_of(j * tk, tk), tk), lanes]

        def qk(k_rows, c_lo, c_hi):
            return lax.dot_general(k_rows, q_ref[c_lo:c_hi, lanes], NT_DIMS,
                                   preferred_element_type=F32)

        for c in range(c0, tq, tk):
            if causal and c == c0:
                s_ref[:half, c:c + half] = qk(k_blk[:half], c, c + half)
                s_ref[:, c + half:c + tk] = qk(k_blk, c + half, c + tk)
            else:
                s_ref[:, c:c + tk] = qk(k_blk, c, c + tk)

    def fold(s, v_rows, cols):
        m_prev = m_ref[:, cols]
        m_new = jnp.maximum(m_prev, jnp.max(s, axis=0, keepdims=True))
        alpha = jnp.exp2(m_prev - m_new)
        p = jnp.exp2(s - m_new).astype(BF16)
        pv = jnp.dot(v_rows, p, preferred_element_type=F32)
        acc_ref[:, cols] = alpha * acc_ref[:, cols] + pv
        m_ref[:, cols] = m_new

    def update(s_ref, hh, j, c0=0, causal=False):
        v_ext = jnp.concatenate([vt_ref[hh, j], jnp.ones((ONES_ROWS, tk), BF16)], axis=0)
        for c in range(c0, tq, tk):
            if causal and c == c0:
                kv = lax.broadcasted_iota(jnp.int32, (half, half), 0)
                qq = lax.broadcasted_iota(jnp.int32, (half, half), 1)
                tri = kv <= qq
                fold(jnp.where(tri, s_ref[:half, c:c + half], -jnp.inf), v_ext[:, :half],
                     slice(c, c + half))
                late = jnp.where(tri, s_ref[half:, c + half:c + tk], -jnp.inf)
                fold(jnp.concatenate([s_ref[:half, c + half:c + tk], late], axis=0), v_ext,
                     slice(c + half, c + tk))
            else:
                fold(s_ref[:, c:c + tk], v_ext, slice(c, c + tk))

    scores_into(s0_ref, 0, 0)
    for hh in range(heads):
        m_ref[...] = jnp.full(m_ref.shape, -jnp.inf, F32)
        acc_ref[...] = jnp.zeros(acc_ref.shape, F32)

        def full_blocks(t, carry, hh=hh):
            for u in range(n_diag):
                scores_into(bufs[(u + 1) % 2], hh, n_diag * t + u + 1)
                update(bufs[u % 2], hh, n_diag * t + u)
            return carry

        lax.fori_loop(0, i, full_blocks, 0)

        for u in range(n_diag):
            if u + 1 < n_diag:
                scores_into(bufs[(u + 1) % 2], hh, n_diag * i + u + 1, c0=(u + 1) * tk,
                            causal=True)
            elif hh + 1 < heads:
                scores_into(bufs[0], hh + 1, 0)
            update(bufs[u % 2], hh, n_diag * i + u, c0=u * tk, causal=True)

        o_ref[:, hh * VDIM:(hh + 1) * VDIM] = (
            acc_ref[:VDIM, :] / acc_ref[VDIM:VDIM + 1, :]).T.astype(o_ref.dtype)


def _flash(q, k, vt):
    tq, tk, heads = 2048, KV_BLOCK, 2
    s = q.shape[0]
    return pl.pallas_call(
        functools.partial(_flash_kernel, tq=tq, tk=tk, heads=heads),
        grid=(HEADS // heads, s // tq),
        in_specs=[pl.BlockSpec((tq, heads * HEAD_SLAB), lambda h, i: (i, h)),
                  pl.BlockSpec((s, heads * HEAD_SLAB), lambda h, i: (0, h)),
                  pl.BlockSpec((heads, s // tk, VDIM, tk), lambda h, i: (h, 0, 0, 0))],
        out_specs=pl.BlockSpec((tq, heads * VDIM), lambda h, i: (i, h)),
        out_shape=jax.ShapeDtypeStruct((s, HEADS * VDIM), BF16),
        scratch_shapes=[pltpu.VMEM((tk, tq), F32), pltpu.VMEM((tk, tq), F32),
                        pltpu.VMEM((1, tq), F32), pltpu.VMEM((VDIM + ONES_ROWS, tq), F32)],
        compiler_params=pltpu.CompilerParams(
            dimension_semantics=("parallel", "arbitrary"), vmem_limit_bytes=VMEM_LIMIT),
        name="flash",
    )(q, k, vt)


def _out_kernel(a_ref, p_ref, x_ref, gate_ref, sg_ref, ws_ref, bs_ref, wo_ref, fg_ref, o_ref,
                mix_ref, wo_bf_ref, *, tm):
    zm_ref, u_ref, v_ref, zs_ref, gm_ref, gs_ref = (
        p_ref.at[:, f * D_MODEL:(f + 1) * D_MODEL] for f in range(BIG_WIDTH // D_MODEL))

    @pl.when(pl.program_id(0) == 0)
    def _():
        wo_bf_ref[...] = wo_ref[...].astype(BF16)

    rows = lax.broadcasted_iota(jnp.int32, (CHUNK, CHUNK), 0)
    cols = lax.broadcasted_iota(jnp.int32, (CHUNK, CHUNK), 1)
    causal = cols <= rows
    n_chunks = tm // CHUNK

    vn = _rms(v_ref[...].astype(F32), sg_ref[...]).astype(BF16)
    for g in range(GROUPS):
        w = jnp.where(causal, ws_ref[g], 0.0).astype(BF16)
        col = slice(g * CHUNK, (g + 1) * CHUNK)
        rhs = jnp.concatenate([vn[c * CHUNK:(c + 1) * CHUNK, col] for c in range(n_chunks)],
                              axis=1)
        mixed = jnp.dot(w, rhs, preferred_element_type=F32)
        for c in range(n_chunks):
            mix_ref[c * CHUNK:(c + 1) * CHUNK, col] = (mixed[:, c * CHUNK:(c + 1) * CHUNK]
                                                       + bs_ref[:, col])

    def gating(g_ref, z_ref):
        z = z_ref[...]
        return ((jnp.tanh(g_ref[...]) + 1.0) * (jnp.tanh(z) + 1.0) * z).astype(F32)

    y_mla = gating(gm_ref, zm_ref) * a_ref[...].astype(F32)
    y_sgu = gating(gs_ref, zs_ref) * u_ref[...].astype(F32) * mix_ref[...]
    twice_merged = (y_mla + y_sgu).astype(BF16)
    y = jnp.dot(twice_merged, wo_bf_ref[...], preferred_element_type=F32)
    o_ref[...] = _rms(x_ref[...] + (0.5 * gate_ref[...]) * y, fg_ref[...])


def _outproj(attn, proj, x2, gate, sg, ws, bs_full, wo, fg):
    tm = 256
    s = x2.shape[0]
    row = lambda i: (i, 0)
    return pl.pallas_call(
        functools.partial(_out_kernel, tm=tm),
        grid=(s // tm,),
        in_specs=[pl.BlockSpec((tm, D_MODEL), row),
                  pl.BlockSpec((tm, BIG_WIDTH), row),
                  pl.BlockSpec((tm, D_MODEL), row), _resident(gate.shape), _resident(sg.shape),
                  _resident(ws.shape), _resident(bs_full.shape), _resident(wo.shape),
                  _resident(fg.shape)],
        out_specs=pl.BlockSpec((tm, D_MODEL), row),
        out_shape=jax.ShapeDtypeStruct((s, D_MODEL), F32),
        scratch_shapes=[pltpu.VMEM((tm, D_MODEL), F32), pltpu.VMEM((D_MODEL, D_MODEL), BF16)],
        compiler_params=pltpu.CompilerParams(
            dimension_semantics=("arbitrary",), vmem_limit_bytes=VMEM_LIMIT),
        name="outproj",
    )(attn, proj, x2, gate, sg, ws, bs_full, wo, fg)


def _rope_tables(positions):
    n_freq = ROPE // 2
    per_row = LANE // n_freq
    inv_freq = 1.0 / (ROPE_THETA ** (jnp.arange(0, ROPE, 2, dtype=F32) / ROPE))
    ang = (positions.astype(F32).reshape(-1, per_row, 1) * inv_freq).reshape(-1, LANE)
    cos = jnp.tile(jnp.cos(ang).reshape(-1, n_freq), (1, per_row))
    sin = jnp.tile(jnp.sin(ang).reshape(-1, n_freq), (1, per_row))
    first = (jnp.arange(LANE) % ROPE) < n_freq
    return jnp.concatenate([cos, jnp.where(first, -sin, 0.0), jnp.where(first, 0.0, sin)], axis=1)


def kernel(x, c, positions, attn_norm_g, w_ada, b_ada, w_in, q_norm_g, w_uq, kv_norm_g, w_ukv,
           sgu_norm_g, w_spatial, b_spatial, w_out, final_norm_g):
    b, s, d = x.shape
    assert (b, s, d) == (1, SEQ, D_MODEL) and attn_norm_g.shape[0] == 1
    x2 = x.reshape(s, d)

    mod = _adaln(jnp.broadcast_to(c, (8, d)), w_ada[0], b_ada[0].reshape(1, 3 * d))[0:1]
    shift, scale, gate = mod[:, :d], mod[:, d:2 * d], mod[:, 2 * d:]

    w_t = jnp.swapaxes(w_in[0], 0, 1)
    wqn, wqr, wk, wvt = _regroup_up_weights(w_uq[0], w_ukv[0])
    h, q, k, vt = _front(x2, attn_norm_g[0].reshape(1, d), scale, shift, w_t,
                         q_norm_g[0].reshape(1, Q_RANK), kv_norm_g[0].reshape(1, KV_RANK),
                         wqn, wqr, wk, wvt, _rope_tables(positions[0]))

    proj = _inproj(h, w_t)
    attn = _flash(q, k, vt)

    bs_full = jnp.repeat(b_spatial[0].T, CHUNK, axis=1)
    out = _outproj(attn, proj, x2, gate, sgu_norm_g[0].reshape(1, d), w_spatial[0], bs_full,
                   w_out[0], final_norm_g.reshape(1, d))
    return out.reshape(b, s, d)
```

```python
import functools
import math

import jax
import jax.numpy as jnp
from jax import lax
from jax.experimental import pallas as pl
from jax.experimental.pallas import tpu as pltpu

D_MODEL = 2048
SEQ = 8192
HEADS = 16
NOPE = 128
ROPE = 64
VDIM = 128
Q_RANK = 768
KV_RANK = 512
ROPE_THETA = 10000.0
GROUPS = 16
CHUNK = 128
EPS = 1e-6

LANE = 128
HEAD_SLAB = 2 * LANE
KV_BLOCK = 512
N_LAT = Q_RANK + KV_RANK + ROPE
LAT_WIDTH = N_LAT + ROPE
BIG_WIDTH = 6 * D_MODEL
VMEM_LIMIT = 56 * 1024 * 1024

Q_SCALE = (1.0 / math.sqrt(NOPE + ROPE)) * math.log2(math.e)

BF16 = jnp.bfloat16
F32 = jnp.float32
NT_DIMS = (((1,), (1,)), ((), ()))


def _sigmoid(x):
    return 0.5 * jnp.tanh(0.5 * x) + 0.5


def _rms(x, g):
    ms = jnp.mean(x * x, axis=-1, keepdims=True)
    return x * lax.rsqrt(ms + EPS) * g


def _resident(shape):
    return pl.BlockSpec(shape, lambda *_: (0,) * len(shape), pipeline_mode=pl.Buffered(1))


def _adaln_kernel(c_ref, w_ref, b_ref, o_ref):
    c = c_ref[...]
    c_act = c * _sigmoid(c)
    o_ref[...] = jnp.dot(c_act.astype(BF16), w_ref[...].astype(BF16),
                         preferred_element_type=F32) + b_ref[...]


def _adaln(c8, w_ada, b_ada):
    tn = 2048
    n = w_ada.shape[1]
    return pl.pallas_call(
        _adaln_kernel,
        grid=(n // tn,),
        in_specs=[pl.BlockSpec((8, D_MODEL), lambda j: (0, 0)),
                  pl.BlockSpec((D_MODEL, tn), lambda j: (0, j)),
                  pl.BlockSpec((1, tn), lambda j: (0, j))],
        out_specs=pl.BlockSpec((8, tn), lambda j: (0, j)),
        out_shape=jax.ShapeDtypeStruct((8, n), F32),
        compiler_params=pltpu.CompilerParams(
            dimension_semantics=("arbitrary",), vmem_limit_bytes=VMEM_LIMIT),
        name="adaln",
    )(c8, w_ada, b_ada)


def _regroup_kernel(wuq_ref, wukv_ref, wqn_ref, wqr_ref, wk_ref, wvt_ref):
    x0, x1, x2 = (wuq_ref[:, k * LANE:(k + 1) * LANE] for k in range(3))
    first = lax.broadcasted_iota(jnp.int32, x0.shape, 1) < ROPE
    nope_b = jnp.where(first, pltpu.roll(x1, ROPE, 1), pltpu.roll(x2, ROPE, 1))
    wqn_ref[:, :LANE] = x0.astype(BF16)
    wqn_ref[:, LANE:] = nope_b.astype(BF16)
    wqr_ref[...] = jnp.where(first, x1, x2).astype(BF16)
    for hd in range(2):
        lo = hd * (NOPE + VDIM)
        wk_ref[:, hd * NOPE:(hd + 1) * NOPE] = wukv_ref[:, lo:lo + NOPE].astype(BF16)
        wvt_ref[hd * VDIM:(hd + 1) * VDIM, :] = (
            wukv_ref[:, lo + NOPE:lo + NOPE + VDIM].T.astype(BF16))


def _regroup_up_weights(w_uq, w_ukv):
    pairs = HEADS // 2
    col = lambda p: (0, p)
    return pl.pallas_call(
        _regroup_kernel,
        grid=(pairs,),
        in_specs=[pl.BlockSpec((Q_RANK, 2 * (NOPE + ROPE)), col),
                  pl.BlockSpec((KV_RANK, 2 * (NOPE + VDIM)), col)],
        out_specs=[pl.BlockSpec((Q_RANK, 2 * NOPE), col), pl.BlockSpec((Q_RANK, 2 * ROPE), col),
                   pl.BlockSpec((KV_RANK, 2 * NOPE), col),
                   pl.BlockSpec((2 * VDIM, KV_RANK), lambda p: (p, 0))],
        out_shape=[jax.ShapeDtypeStruct((Q_RANK, HEADS * NOPE), BF16),
                   jax.ShapeDtypeStruct((Q_RANK, HEADS * ROPE), BF16),
                   jax.ShapeDtypeStruct((KV_RANK, HEADS * NOPE), BF16),
                   jax.ShapeDtypeStruct((HEADS * VDIM, KV_RANK), BF16)],
        compiler_params=pltpu.CompilerParams(
            dimension_semantics=("parallel",), vmem_limit_bytes=VMEM_LIMIT),
        name="regroup",
    )(w_uq, w_ukv)


def _rope(t, cos4, sin_a, sin_b):
    return t * cos4 + pltpu.roll(t, 96, 1) * sin_a + pltpu.roll(t, 32, 1) * sin_b


def _latent_kernel(x_ref, g_ref, scale_ref, shift_ref, wlat_ref, h_ref, lat_ref, wlat_bf_ref):
    @pl.when(pl.program_id(0) == 0)
    def _():
        wlat_bf_ref[...] = wlat_ref[...].astype(BF16)

    y = _rms(x_ref[...], g_ref[...])
    h = (y * (1.0 + scale_ref[...]) + shift_ref[...]).astype(BF16)
    h_ref[...] = h
    lat_ref[...] = lax.dot_general(h, wlat_bf_ref[...], NT_DIMS,
                                   preferred_element_type=F32).astype(lat_ref.dtype)


def _front_kernel(lat_ref, qg_ref, kvg_ref, wqn_ref, wqr_ref, wk_ref, wvt_ref, cos_ref, sa_ref,
                  sb_ref, q_ref, k_ref, vt_ref):
    lat = lat_ref[...].astype(F32)
    cos4, sin_a, sin_b = cos_ref[...], sa_ref[...], sb_ref[...]
    low_half = lax.broadcasted_iota(jnp.int32, cos4.shape, 1) < ROPE

    qn = _rms(lat[:, :Q_RANK], qg_ref[...]).astype(BF16)
    q_nope = jnp.dot(qn, wqn_ref[...], preferred_element_type=F32)
    q_rope = jnp.dot(qn, wqr_ref[...], preferred_element_type=F32)
    for pair in range(HEADS // 2):
        rot = _rope(q_rope[:, pair * LANE:(pair + 1) * LANE], cos4, sin_a, sin_b) * Q_SCALE
        for odd, slab in enumerate((rot, pltpu.roll(rot, ROPE, 1))):
            lo = (2 * pair + odd) * HEAD_SLAB
            q_ref[:, lo + LANE:lo + HEAD_SLAB] = jnp.where(low_half, slab, 0.0).astype(BF16)
    for hd in range(HEADS):
        lo = hd * HEAD_SLAB
        q_ref[:, lo:lo + LANE] = (q_nope[:, hd * LANE:(hd + 1) * LANE] * Q_SCALE).astype(BF16)

    kvn = _rms(lat[:, Q_RANK:Q_RANK + KV_RANK], kvg_ref[...]).astype(BF16)
    vt = lax.dot_general(wvt_ref[...], kvn, NT_DIMS, preferred_element_type=F32)
    vt_ref[...] = vt.astype(BF16).reshape(vt_ref.shape)
    kn = jnp.dot(kvn, wk_ref[...], preferred_element_type=F32)
    kr = _rope(lat[:, Q_RANK + KV_RANK:Q_RANK + KV_RANK + LANE], cos4, sin_a, sin_b)
    kr = jnp.where(low_half, kr, 0.0).astype(BF16)
    for hd in range(HEADS):
        lo = hd * HEAD_SLAB
        k_ref[:, lo:lo + LANE] = kn[:, hd * LANE:(hd + 1) * LANE].astype(BF16)
        k_ref[:, lo + LANE:lo + HEAD_SLAB] = kr


def _latent(x2, g, scale, shift, w_t):
    tm = 1024
    s = x2.shape[0]
    row = lambda i: (i, 0)
    wlat_spec = pl.BlockSpec((pl.Element(LAT_WIDTH), pl.Element(D_MODEL)), lambda i: (0, 0),
                             pipeline_mode=pl.Buffered(1))
    return pl.pallas_call(
        _latent_kernel,
        grid=(s // tm,),
        in_specs=[pl.BlockSpec((tm, D_MODEL), row),
                  _resident(g.shape), _resident(scale.shape), _resident(shift.shape), wlat_spec],
        out_specs=[pl.BlockSpec((tm, D_MODEL), row), pl.BlockSpec((tm, LAT_WIDTH), row)],
        out_shape=[jax.ShapeDtypeStruct((s, D_MODEL), BF16),
                   jax.ShapeDtypeStruct((s, LAT_WIDTH), BF16)],
        scratch_shapes=[pltpu.VMEM((LAT_WIDTH, D_MODEL), BF16)],
        compiler_params=pltpu.CompilerParams(
            dimension_semantics=("arbitrary",), vmem_limit_bytes=VMEM_LIMIT),
        name="latent",
    )(x2, g, scale, shift, w_t)


def _front(lat, qg, kvg, wqn, wqr, wk, wvt, cos4, sin_a, sin_b):
    tm = KV_BLOCK
    s = lat.shape[0]
    row = lambda i: (i, 0)
    tab = pl.BlockSpec((tm, LANE), row)
    slab = pl.BlockSpec((tm, HEADS * HEAD_SLAB), row)
    vt_spec = pl.BlockSpec((HEADS, 1, VDIM, tm), lambda i: (0, i, 0, 0))
    return pl.pallas_call(
        _front_kernel,
        grid=(s // tm,),
        in_specs=[pl.BlockSpec((tm, LAT_WIDTH), row), _resident(qg.shape), _resident(kvg.shape),
                  _resident(wqn.shape), _resident(wqr.shape), _resident(wk.shape),
                  _resident(wvt.shape), tab, tab, tab],
        out_specs=[slab, slab, vt_spec],
        out_shape=[jax.ShapeDtypeStruct((s, HEADS * HEAD_SLAB), BF16),
                   jax.ShapeDtypeStruct((s, HEADS * HEAD_SLAB), BF16),
                   jax.ShapeDtypeStruct((HEADS, s // KV_BLOCK, VDIM, KV_BLOCK), BF16)],
        compiler_params=pltpu.CompilerParams(
            dimension_semantics=("parallel",), vmem_limit_bytes=VMEM_LIMIT),
        name="front",
    )(lat, qg, kvg, wqn, wqr, wk, wvt, cos4, sin_a, sin_b)


INPROJ_TN = 1536
N_BIG_TILES = BIG_WIDTH // INPROJ_TN


def _inproj_kernel(h_ref, wt_ref, o_ref, wb_ref):
    @pl.when(pl.program_id(1) == 0)
    def _():
        rows = lax.broadcasted_iota(jnp.int32, (INPROJ_TN, 1), 0)
        feat = pl.program_id(0) * INPROJ_TN + rows
        halved = (feat < D_MODEL) | (feat >= 3 * D_MODEL)
        wb_ref[...] = (wt_ref[...] * jnp.where(halved, 0.5, 1.0)).astype(BF16)

    o_ref[...] = lax.dot_general(h_ref[...], wb_ref[...], NT_DIMS,
                                 preferred_element_type=F32).astype(o_ref.dtype)


def _inproj(h, w_t):
    tm, tn = 1024, INPROJ_TN
    s = h.shape[0]
    sub = 8
    w_spec = pl.BlockSpec((pl.Element(tn), pl.Element(D_MODEL)),
                          lambda j, i: ((N_LAT // sub + j * (tn // sub)) * sub, 0))
    return pl.pallas_call(
        _inproj_kernel,
        grid=(N_BIG_TILES, s // tm),
        in_specs=[pl.BlockSpec((tm, D_MODEL), lambda j, i: (i, 0)), w_spec],
        out_specs=pl.BlockSpec((tm, tn), lambda j, i: (i, j)),
        out_shape=jax.ShapeDtypeStruct((s, BIG_WIDTH), BF16),
        scratch_shapes=[pltpu.VMEM((tn, D_MODEL), BF16)],
        compiler_params=pltpu.CompilerParams(
            dimension_semantics=("arbitrary", "arbitrary"), vmem_limit_bytes=VMEM_LIMIT),
        name="inproj",
    )(h, w_t)


ONES_ROWS = 16


def _flash_kernel(q_ref, k_ref, vt_ref, o_ref, s0_ref, s1_ref, m_ref, acc_ref, *, tq, tk, heads):
    i = pl.program_id(1)
    n_diag = tq // tk
    bufs = (s0_ref, s1_ref)
    assert n_diag % 2 == 0

    half = tk // 2

    def scores_into(s_ref, hh, j, c0=0, causal=False):
        lanes = slice(hh * HEAD_SLAB, (hh + 1) * HEAD_SLAB)
        k_blk = k_ref[pl.ds(pl.multiple_of(j * tk, tk), tk), lanes]

        def qk(k_rows, c_lo, c_hi):
            return lax.dot_general(k_rows, q_ref[c_lo:c_hi, lanes], NT_DIMS,
                                   preferred_element_type=F32)

        for c in range(c0, tq, tk):
            if causal and c == c0:
                s_ref[:half, c:c + half] = qk(k_blk[:half], c, c + half)
                s_ref[:, c + half:c + tk] = qk(k_blk, c + half, c + tk)
            else:
                s_ref[:, c:c + tk] = qk(k_blk, c, c + tk)

    def fold(s, v_rows, cols):
        m_prev = m_ref[:, cols]
        m_new = jnp.maximum(m_prev, jnp.max(s, axis=0, keepdims=True))
        alpha = jnp.exp2(m_prev - m_new)
        p = jnp.exp2(s - m_new).astype(BF16)
        pv = jnp.dot(v_rows, p, preferred_element_type=F32)
        acc_ref[:, cols] = alpha * acc_ref[:, cols] + pv
        m_ref[:, cols] = m_new

    def update(s_ref, hh, j, c0=0, causal=False):
        v_ext = jnp.concatenate([vt_ref[hh, j], jnp.ones((ONES_ROWS, tk), BF16)], axis=0)
        for c in range(c0, tq, tk):
            if causal and c == c0:
                kv = lax.broadcasted_iota(jnp.int32, (half, half), 0)
                qq = lax.broadcasted_iota(jnp.int32, (half, half), 1)
                tri = kv <= qq
                fold(jnp.where(tri, s_ref[:half, c:c + half], -jnp.inf), v_ext[:, :half],
                     slice(c, c + half))
                late = jnp.where(tri, s_ref[half:, c + half:c + tk], -jnp.inf)
                fold(jnp.concatenate([s_ref[:half, c + half:c + tk], late], axis=0), v_ext,
                     slice(c + half, c + tk))
            else:
                fold(s_ref[:, c:c + tk], v_ext, slice(c, c + tk))

    scores_into(s0_ref, 0, 0)
    for hh in range(heads):
        m_ref[...] = jnp.full(m_ref.shape, -jnp.inf, F32)
        acc_ref[...] = jnp.zeros(acc_ref.shape, F32)

        def full_blocks(t, carry, hh=hh):
            for u in range(n_diag):
                scores_into(bufs[(u + 1) % 2], hh, n_diag * t + u + 1)
                update(bufs[u % 2], hh, n_diag * t + u)
            return carry

        lax.fori_loop(0, i, full_blocks, 0)

        for u in range(n_diag):
            if u + 1 < n_diag:
                scores_into(bufs[(u + 1) % 2], hh, n_diag * i + u + 1, c0=(u + 1) * tk,
                            causal=True)
            elif hh + 1 < heads:
                scores_into(bufs[0], hh + 1, 0)
            update(bufs[u % 2], hh, n_diag * i + u, c0=u * tk, causal=True)

        o_ref[:, hh * VDIM:(hh + 1) * VDIM] = (
            acc_ref[:VDIM, :] / acc_ref[VDIM:VDIM + 1, :]).T.astype(o_ref.dtype)


def _flash(q, k, vt):
    tq, tk, heads = 2048, KV_BLOCK, 2
    s = q.shape[0]
    return pl.pallas_call(
        functools.partial(_flash_kernel, tq=tq, tk=tk, heads=heads),
        grid=(HEADS // heads, s // tq),
        in_specs=[pl.BlockSpec((tq, heads * HEAD_SLAB), lambda h, i: (i, h)),
                  pl.BlockSpec((s, heads * HEAD_SLAB), lambda h, i: (0, h)),
                  pl.BlockSpec((heads, s // tk, VDIM, tk), lambda h, i: (h, 0, 0, 0))],
        out_specs=pl.BlockSpec((tq, heads * VDIM), lambda h, i: (i, h)),
        out_shape=jax.ShapeDtypeStruct((s, HEADS * VDIM), BF16),
        scratch_shapes=[pltpu.VMEM((tk, tq), F32), pltpu.VMEM((tk, tq), F32),
                        pltpu.VMEM((1, tq), F32), pltpu.VMEM((VDIM + ONES_ROWS, tq), F32)],
        compiler_params=pltpu.CompilerParams(
            dimension_semantics=("parallel", "arbitrary"), vmem_limit_bytes=VMEM_LIMIT),
        name="flash",
    )(q, k, vt)


def _out_kernel(a_ref, zm_ref, u_ref, v_ref, zs_ref, gm_ref, gs_ref, x_ref, gate_ref,
                sg_ref, ws_ref, bs_ref, wo_ref, fg_ref, o_ref, mix_ref, wo_bf_ref, *, tm):
    @pl.when(pl.program_id(0) == 0)
    def _():
        wo_bf_ref[...] = wo_ref[...].astype(BF16)

    rows = lax.broadcasted_iota(jnp.int32, (CHUNK, CHUNK), 0)
    cols = lax.broadcasted_iota(jnp.int32, (CHUNK, CHUNK), 1)
    causal = cols <= rows
    n_chunks = tm // CHUNK

    vn = _rms(v_ref[...].astype(F32), sg_ref[...]).astype(BF16)
    for g in range(GROUPS):
        w = jnp.where(causal, ws_ref[g], 0.0).astype(BF16)
        col = slice(g * CHUNK, (g + 1) * CHUNK)
        rhs = jnp.concatenate([vn[c * CHUNK:(c + 1) * CHUNK, col] for c in range(n_chunks)],
                              axis=1)
        mixed = jnp.dot(w, rhs, preferred_element_type=F32)
        for c in range(n_chunks):
            mix_ref[c * CHUNK:(c + 1) * CHUNK, col] = (mixed[:, c * CHUNK:(c + 1) * CHUNK]
                                                       + bs_ref[:, col])

    def gating(g_ref, z_ref):
        z = z_ref[...]
        return ((jnp.tanh(g_ref[...]) + 1.0) * (jnp.tanh(z) + 1.0) * z).astype(F32)

    y_mla = gating(gm_ref, zm_ref) * a_ref[...].astype(F32)
    y_sgu = gating(gs_ref, zs_ref) * u_ref[...].astype(F32) * mix_ref[...]
    twice_merged = (y_mla + y_sgu).astype(BF16)
    y = jnp.dot(twice_merged, wo_bf_ref[...], preferred_element_type=F32)
    o_ref[...] = _rms(x_ref[...] + (0.5 * gate_ref[...]) * y, fg_ref[...])


def _outproj(attn, proj, x2, gate, sg, ws, bs_full, wo, fg):
    tm = 256
    s = x2.shape[0]
    row = lambda i: (i, 0)
    big = lambda k: pl.BlockSpec((tm, D_MODEL), lambda i, k=k: (i, k))
    return pl.pallas_call(
        functools.partial(_out_kernel, tm=tm),
        grid=(s // tm,),
        in_specs=[pl.BlockSpec((tm, D_MODEL), row),
                  big(0), big(1), big(2), big(3), big(4), big(5),
                  pl.BlockSpec((tm, D_MODEL), row), _resident(gate.shape), _resident(sg.shape),
                  _resident(ws.shape), _resident(bs_full.shape), _resident(wo.shape),
                  _resident(fg.shape)],
        out_specs=pl.BlockSpec((tm, D_MODEL), row),
        out_shape=jax.ShapeDtypeStruct((s, D_MODEL), F32),
        scratch_shapes=[pltpu.VMEM((tm, D_MODEL), F32), pltpu.VMEM((D_MODEL, D_MODEL), BF16)],
        compiler_params=pltpu.CompilerParams(
            dimension_semantics=("arbitrary",), vmem_limit_bytes=VMEM_LIMIT),
        name="outproj",
    )(attn, proj, proj, proj, proj, proj, proj, x2, gate, sg, ws, bs_full, wo, fg)


def _rope_tables(positions):
    n_freq = ROPE // 2
    per_row = LANE // n_freq
    inv_freq = 1.0 / (ROPE_THETA ** (jnp.arange(0, ROPE, 2, dtype=F32) / ROPE))
    ang = (positions.astype(F32).reshape(-1, per_row, 1) * inv_freq).reshape(-1, LANE)
    cos = jnp.tile(jnp.cos(ang).reshape(-1, n_freq), (1, per_row))
    sin = jnp.tile(jnp.sin(ang).reshape(-1, n_freq), (1, per_row))
    first = (jnp.arange(LANE) % ROPE) < n_freq
    return cos, jnp.where(first, -sin, 0.0), jnp.where(first, 0.0, sin)


def kernel(x, c, positions, attn_norm_g, w_ada, b_ada, w_in, q_norm_g, w_uq, kv_norm_g, w_ukv,
           sgu_norm_g, w_spatial, b_spatial, w_out, final_norm_g):
    b, s, d = x.shape
    assert (b, s, d) == (1, SEQ, D_MODEL) and attn_norm_g.shape[0] == 1
    x2 = x.reshape(s, d)

    mod = _adaln(jnp.broadcast_to(c, (8, d)), w_ada[0], b_ada[0].reshape(1, 3 * d))[0:1]
    shift, scale, gate = mod[:, :d], mod[:, d:2 * d], mod[:, 2 * d:]

    w_t = jnp.swapaxes(w_in[0], 0, 1)
    wqn, wqr, wk, wvt = _regroup_up_weights(w_uq[0], w_ukv[0])
    h, lat = _latent(x2, attn_norm_g[0].reshape(1, d), scale, shift, w_t)
    q, k, vt = _front(lat, q_norm_g[0].reshape(1, Q_RANK), kv_norm_g[0].reshape(1, KV_RANK),
                      wqn, wqr, wk, wvt, *_rope_tables(positions[0]))

    proj = _inproj(h, w_t)
    attn = _flash(q, k, vt)

    bs_full = jnp.repeat(b_spatial[0].T, CHUNK, axis=1)
    out = _outproj(attn, proj, x2, gate, sgu_norm_g[0].reshape(1, d), w_spatial[0], bs_full,
                   w_out[0], final_norm_g.reshape(1, d))
    return out.reshape(b, s, d)
```

```python
import functools
import math

import jax
import jax.numpy as jnp
from jax import lax
from jax.experimental import pallas as pl
from jax.experimental.pallas import tpu as pltpu

D_MODEL = 2048
SEQ = 8192
HEADS = 16
NOPE = 128
ROPE = 64
VDIM = 128
Q_RANK = 768
KV_RANK = 512
ROPE_THETA = 10000.0
GROUPS = 16
CHUNK = 128
EPS = 1e-6

LANE = 128
HEAD_SLAB = 2 * LANE
KV_BLOCK = 512
N_LAT = Q_RANK + KV_RANK + ROPE
LAT_WIDTH = N_LAT + ROPE
BIG_WIDTH = 6 * D_MODEL
VMEM_LIMIT = 56 * 1024 * 1024

Q_SCALE = (1.0 / math.sqrt(NOPE + ROPE)) * math.log2(math.e)

BF16 = jnp.bfloat16
F32 = jnp.float32
NT_DIMS = (((1,), (1,)), ((), ()))


def _sigmoid(x):
    return 0.5 * jnp.tanh(0.5 * x) + 0.5


def _rms(x, g):
    ms = jnp.mean(x * x, axis=-1, keepdims=True)
    return x * lax.rsqrt(ms + EPS) * g


def _resident(shape):
    return pl.BlockSpec(shape, lambda *_: (0,) * len(shape), pipeline_mode=pl.Buffered(1))


def _adaln_kernel(c_ref, w_ref, b_ref, o_ref):
    c = c_ref[...]
    c_act = c * _sigmoid(c)
    o_ref[...] = jnp.dot(c_act.astype(BF16), w_ref[...].astype(BF16),
                         preferred_element_type=F32) + b_ref[...]


def _adaln(c8, w_ada, b_ada):
    tn = 2048
    n = w_ada.shape[1]
    return pl.pallas_call(
        _adaln_kernel,
        grid=(n // tn,),
        in_specs=[pl.BlockSpec((8, D_MODEL), lambda j: (0, 0)),
                  pl.BlockSpec((D_MODEL, tn), lambda j: (0, j)),
                  pl.BlockSpec((1, tn), lambda j: (0, j))],
        out_specs=pl.BlockSpec((8, tn), lambda j: (0, j)),
        out_shape=jax.ShapeDtypeStruct((8, n), F32),
        compiler_params=pltpu.CompilerParams(
            dimension_semantics=("arbitrary",), vmem_limit_bytes=VMEM_LIMIT),
        name="adaln",
    )(c8, w_ada, b_ada)


def _regroup_kernel(wuq_ref, wukv_ref, wqn_ref, wqr_ref, wk_ref, wvt_ref):
    x0, x1, x2 = (wuq_ref[:, k * LANE:(k + 1) * LANE] for k in range(3))
    first = lax.broadcasted_iota(jnp.int32, x0.shape, 1) < ROPE
    nope_b = jnp.where(first, pltpu.roll(x1, ROPE, 1), pltpu.roll(x2, ROPE, 1))
    wqn_ref[:, :LANE] = x0.astype(BF16)
    wqn_ref[:, LANE:] = nope_b.astype(BF16)
    wqr_ref[...] = jnp.where(first, x1, x2).astype(BF16)
    for hd in range(2):
        lo = hd * (NOPE + VDIM)
        wk_ref[:, hd * NOPE:(hd + 1) * NOPE] = wukv_ref[:, lo:lo + NOPE].astype(BF16)
        wvt_ref[hd * VDIM:(hd + 1) * VDIM, :] = (
            wukv_ref[:, lo + NOPE:lo + NOPE + VDIM].T.astype(BF16))


def _regroup_up_weights(w_uq, w_ukv):
    pairs = HEADS // 2
    col = lambda p: (0, p)
    return pl.pallas_call(
        _regroup_kernel,
        grid=(pairs,),
        in_specs=[pl.BlockSpec((Q_RANK, 2 * (NOPE + ROPE)), col),
                  pl.BlockSpec((KV_RANK, 2 * (NOPE + VDIM)), col)],
        out_specs=[pl.BlockSpec((Q_RANK, 2 * NOPE), col), pl.BlockSpec((Q_RANK, 2 * ROPE), col),
                   pl.BlockSpec((KV_RANK, 2 * NOPE), col),
                   pl.BlockSpec((2 * VDIM, KV_RANK), lambda p: (p, 0))],
        out_shape=[jax.ShapeDtypeStruct((Q_RANK, HEADS * NOPE), BF16),
                   jax.ShapeDtypeStruct((Q_RANK, HEADS * ROPE), BF16),
                   jax.ShapeDtypeStruct((KV_RANK, HEADS * NOPE), BF16),
                   jax.ShapeDtypeStruct((HEADS * VDIM, KV_RANK), BF16)],
        compiler_params=pltpu.CompilerParams(
            dimension_semantics=("parallel",), vmem_limit_bytes=VMEM_LIMIT),
        name="regroup",
    )(w_uq, w_ukv)


def _rope(t, cos4, sin_a, sin_b):
    return t * cos4 + pltpu.roll(t, 96, 1) * sin_a + pltpu.roll(t, 32, 1) * sin_b


def _latent_kernel(x_ref, g_ref, scale_ref, shift_ref, wlat_ref, h_ref, lat_ref, wlat_bf_ref):
    @pl.when(pl.program_id(0) == 0)
    def _():
        wlat_bf_ref[...] = wlat_ref[...].astype(BF16)

    y = _rms(x_ref[...], g_ref[...])
    h = (y * (1.0 + scale_ref[...]) + shift_ref[...]).astype(BF16)
    h_ref[...] = h
    lat_ref[...] = lax.dot_general(h, wlat_bf_ref[...], NT_DIMS,
                                   preferred_element_type=F32).astype(lat_ref.dtype)


def _front_kernel(lat_ref, qg_ref, kvg_ref, wqn_ref, wqr_ref, wk_ref, wvt_ref, cos_ref, sa_ref,
                  sb_ref, q_ref, k_ref, kr_ref, vt_ref):
    lat = lat_ref[...].astype(F32)
    cos4, sin_a, sin_b = cos_ref[...], sa_ref[...], sb_ref[...]
    low_half = lax.broadcasted_iota(jnp.int32, cos4.shape, 1) < ROPE

    qn = _rms(lat[:, :Q_RANK], qg_ref[...]).astype(BF16)
    q_nope = jnp.dot(qn, wqn_ref[...], preferred_element_type=F32)
    q_rope = jnp.dot(qn, wqr_ref[...], preferred_element_type=F32)
    for pair in range(HEADS // 2):
        rot = _rope(q_rope[:, pair * LANE:(pair + 1) * LANE], cos4, sin_a, sin_b) * Q_SCALE
        for odd, slab in enumerate((rot, pltpu.roll(rot, ROPE, 1))):
            lo = (2 * pair + odd) * HEAD_SLAB
            q_ref[:, lo + LANE:lo + HEAD_SLAB] = jnp.where(low_half, slab, 0.0).astype(BF16)
    for hd in range(HEADS):
        lo = hd * HEAD_SLAB
        q_ref[:, lo:lo + LANE] = (q_nope[:, hd * LANE:(hd + 1) * LANE] * Q_SCALE).astype(BF16)

    kvn = _rms(lat[:, Q_RANK:Q_RANK + KV_RANK], kvg_ref[...]).astype(BF16)
    vt = lax.dot_general(wvt_ref[...], kvn, NT_DIMS, preferred_element_type=F32)
    vt_ref[...] = vt.astype(BF16).reshape(vt_ref.shape)
    k_ref[...] = jnp.dot(kvn, wk_ref[...], preferred_element_type=F32).astype(BF16)
    kr = _rope(lat[:, Q_RANK + KV_RANK:Q_RANK + KV_RANK + LANE], cos4, sin_a, sin_b)
    kr_ref[...] = jnp.where(low_half, kr, 0.0).astype(BF16)


def _latent(x2, g, scale, shift, w_t):
    tm = 1024
    s = x2.shape[0]
    row = lambda i: (i, 0)
    wlat_spec = pl.BlockSpec((pl.Element(LAT_WIDTH), pl.Element(D_MODEL)), lambda i: (0, 0),
                             pipeline_mode=pl.Buffered(1))
    return pl.pallas_call(
        _latent_kernel,
        grid=(s // tm,),
        in_specs=[pl.BlockSpec((tm, D_MODEL), row),
                  _resident(g.shape), _resident(scale.shape), _resident(shift.shape), wlat_spec],
        out_specs=[pl.BlockSpec((tm, D_MODEL), row), pl.BlockSpec((tm, LAT_WIDTH), row)],
        out_shape=[jax.ShapeDtypeStruct((s, D_MODEL), BF16),
                   jax.ShapeDtypeStruct((s, LAT_WIDTH), BF16)],
        scratch_shapes=[pltpu.VMEM((LAT_WIDTH, D_MODEL), BF16)],
        compiler_params=pltpu.CompilerParams(
            dimension_semantics=("arbitrary",), vmem_limit_bytes=VMEM_LIMIT),
        name="latent",
    )(x2, g, scale, shift, w_t)


def _front(lat, qg, kvg, wqn, wqr, wk, wvt, cos4, sin_a, sin_b):
    tm = KV_BLOCK
    s = lat.shape[0]
    row = lambda i: (i, 0)
    tab = pl.BlockSpec((tm, LANE), row)
    slab = pl.BlockSpec((tm, HEADS * HEAD_SLAB), row)
    vt_spec = pl.BlockSpec((HEADS, 1, VDIM, tm), lambda i: (0, i, 0, 0))
    return pl.pallas_call(
        _front_kernel,
        grid=(s // tm,),
        in_specs=[pl.BlockSpec((tm, LAT_WIDTH), row), _resident(qg.shape), _resident(kvg.shape),
                  _resident(wqn.shape), _resident(wqr.shape), _resident(wk.shape),
                  _resident(wvt.shape), tab, tab, tab],
        out_specs=[slab, pl.BlockSpec((tm, HEADS * NOPE), row), tab, vt_spec],
        out_shape=[jax.ShapeDtypeStruct((s, HEADS * HEAD_SLAB), BF16),
                   jax.ShapeDtypeStruct((s, HEADS * NOPE), BF16),
                   jax.ShapeDtypeStruct((s, LANE), BF16),
                   jax.ShapeDtypeStruct((HEADS, s // KV_BLOCK, VDIM, KV_BLOCK), BF16)],
        compiler_params=pltpu.CompilerParams(
            dimension_semantics=("parallel",), vmem_limit_bytes=VMEM_LIMIT),
        name="front",
    )(lat, qg, kvg, wqn, wqr, wk, wvt, cos4, sin_a, sin_b)


INPROJ_TN = 1536
N_BIG_TILES = BIG_WIDTH // INPROJ_TN


def _inproj_kernel(h_ref, wt_ref, o_ref, wb_ref):
    @pl.when(pl.program_id(1) == 0)
    def _():
        rows = lax.broadcasted_iota(jnp.int32, (INPROJ_TN, 1), 0)
        feat = pl.program_id(0) * INPROJ_TN + rows
        halved = (feat < D_MODEL) | (feat >= 3 * D_MODEL)
        wb_ref[...] = (wt_ref[...] * jnp.where(halved, 0.5, 1.0)).astype(BF16)

    o_ref[...] = lax.dot_general(h_ref[...], wb_ref[...], NT_DIMS,
                                 preferred_element_type=F32).astype(o_ref.dtype)


def _inproj(h, w_t):
    tm, tn = 1024, INPROJ_TN
    s = h.shape[0]
    sub = 8
    w_spec = pl.BlockSpec((pl.Element(tn), pl.Element(D_MODEL)),
                          lambda j, i: ((N_LAT // sub + j * (tn // sub)) * sub, 0))
    return pl.pallas_call(
        _inproj_kernel,
        grid=(N_BIG_TILES, s // tm),
        in_specs=[pl.BlockSpec((tm, D_MODEL), lambda j, i: (i, 0)), w_spec],
        out_specs=pl.BlockSpec((tm, tn), lambda j, i: (i, j)),
        out_shape=jax.ShapeDtypeStruct((s, BIG_WIDTH), BF16),
        scratch_shapes=[pltpu.VMEM((tn, D_MODEL), BF16)],
        compiler_params=pltpu.CompilerParams(
            dimension_semantics=("arbitrary", "arbitrary"), vmem_limit_bytes=VMEM_LIMIT),
        name="inproj",
    )(h, w_t)


ONES_ROWS = 16


def _flash_kernel(q_ref, k_ref, kr_ref, vt_ref, o_ref, s0_ref, s1_ref, m_ref, acc_ref, *, tq, tk,
                  heads):
    i = pl.program_id(1)
    n_diag = tq // tk
    bufs = (s0_ref, s1_ref)
    assert n_diag % 2 == 0

    half = tk // 2

    def scores_into(s_ref, hh, j, c0=0, causal=False):
        lanes = slice(hh * HEAD_SLAB, (hh + 1) * HEAD_SLAB)
        rows = pl.ds(pl.multiple_of(j * tk, tk), tk)
        k_blk = jnp.concatenate([k_ref[rows, hh * NOPE:(hh + 1) * NOPE], kr_ref[rows, :]], axis=1)

        def qk(k_rows, c_lo, c_hi):
            return lax.dot_general(k_rows, q_ref[c_lo:c_hi, lanes], NT_DIMS,
                                   preferred_element_type=F32)

        for c in range(c0, tq, tk):
            if causal and c == c0:
                s_ref[:half, c:c + half] = qk(k_blk[:half], c, c + half)
                s_ref[:, c + half:c + tk] = qk(k_blk, c + half, c + tk)
            else:
                s_ref[:, c:c + tk] = qk(k_blk, c, c + tk)

    def fold(s, v_rows, cols):
        m_prev = m_ref[:, cols]
        m_new = jnp.maximum(m_prev, jnp.max(s, axis=0, keepdims=True))
        alpha = jnp.exp2(m_prev - m_new)
        p = jnp.exp2(s - m_new).astype(BF16)
        pv = jnp.dot(v_rows, p, preferred_element_type=F32)
        acc_ref[:, cols] = alpha * acc_ref[:, cols] + pv
        m_ref[:, cols] = m_new

    def update(s_ref, hh, j, c0=0, causal=False):
        v_ext = jnp.concatenate([vt_ref[hh, j], jnp.ones((ONES_ROWS, tk), BF16)], axis=0)
        for c in range(c0, tq, tk):
            if causal and c == c0:
                kv = lax.broadcasted_iota(jnp.int32, (half, half), 0)
                qq = lax.broadcasted_iota(jnp.int32, (half, half), 1)
                tri = kv <= qq
                fold(jnp.where(tri, s_ref[:half, c:c + half], -jnp.inf), v_ext[:, :half],
                     slice(c, c + half))
                late = jnp.where(tri, s_ref[half:, c + half:c + tk], -jnp.inf)
                fold(jnp.concatenate([s_ref[:half, c + half:c + tk], late], axis=0), v_ext,
                     slice(c + half, c + tk))
            else:
                fold(s_ref[:, c:c + tk], v_ext, slice(c, c + tk))

    scores_into(s0_ref, 0, 0)
    for hh in range(heads):
        m_ref[...] = jnp.full(m_ref.shape, -jnp.inf, F32)
        acc_ref[...] = jnp.zeros(acc_ref.shape, F32)

        def full_blocks(t, carry, hh=hh):
            for u in range(n_diag):
                scores_into(bufs[(u + 1) % 2], hh, n_diag * t + u + 1)
                update(bufs[u % 2], hh, n_diag * t + u)
            return carry

        lax.fori_loop(0, i, full_blocks, 0)

        for u in range(n_diag):
            if u + 1 < n_diag:
                scores_into(bufs[(u + 1) % 2], hh, n_diag * i + u + 1, c0=(u + 1) * tk,
                            causal=True)
            elif hh + 1 < heads:
                scores_into(bufs[0], hh + 1, 0)
            update(bufs[u % 2], hh, n_diag * i + u, c0=u * tk, causal=True)

        o_ref[:, hh * VDIM:(hh + 1) * VDIM] = (
            acc_ref[:VDIM, :] / acc_ref[VDIM:VDIM + 1, :]).T.astype(o_ref.dtype)


def _flash(q, k, kr, vt):
    tq, tk, heads = 2048, KV_BLOCK, 2
    s = q.shape[0]
    return pl.pallas_call(
        functools.partial(_flash_kernel, tq=tq, tk=tk, heads=heads),
        grid=(HEADS // heads, s // tq),
        in_specs=[pl.BlockSpec((tq, heads * HEAD_SLAB), lambda h, i: (i, h)),
                  pl.BlockSpec((s, heads * NOPE), lambda h, i: (0, h)), _resident(kr.shape),
                  pl.BlockSpec((heads, s // tk, VDIM, tk), lambda h, i: (h, 0, 0, 0))],
        out_specs=pl.BlockSpec((tq, heads * VDIM), lambda h, i: (i, h)),
        out_shape=jax.ShapeDtypeStruct((s, HEADS * VDIM), BF16),
        scratch_shapes=[pltpu.VMEM((tk, tq), F32), pltpu.VMEM((tk, tq), F32),
                        pltpu.VMEM((1, tq), F32), pltpu.VMEM((VDIM + ONES_ROWS, tq), F32)],
        compiler_params=pltpu.CompilerParams(
            dimension_semantics=("parallel", "arbitrary"), vmem_limit_bytes=VMEM_LIMIT),
        name="flash",
    )(q, k, kr, vt)


def _out_kernel(a_ref, zm_ref, u_ref, v_ref, zs_ref, gm_ref, gs_ref, x_ref, gate_ref,
                sg_ref, ws_ref, bs_ref, wo_ref, fg_ref, o_ref, mix_ref, wo_bf_ref, *, tm):
    @pl.when(pl.program_id(0) == 0)
    def _():
        wo_bf_ref[...] = wo_ref[...].astype(BF16)

    rows = lax.broadcasted_iota(jnp.int32, (CHUNK, CHUNK), 0)
    cols = lax.broadcasted_iota(jnp.int32, (CHUNK, CHUNK), 1)
    causal = cols <= rows
    n_chunks = tm // CHUNK

    vn = _rms(v_ref[...].astype(F32), sg_ref[...]).astype(BF16)
    for g in range(GROUPS):
        w = jnp.where(causal, ws_ref[g], 0.0).astype(BF16)
        col = slice(g * CHUNK, (g + 1) * CHUNK)
        rhs = jnp.concatenate([vn[c * CHUNK:(c + 1) * CHUNK, col] for c in range(n_chunks)],
                              axis=1)
        mixed = jnp.dot(w, rhs, preferred_element_type=F32)
        for c in range(n_chunks):
            mix_ref[c * CHUNK:(c + 1) * CHUNK, col] = (mixed[:, c * CHUNK:(c + 1) * CHUNK]
                                                       + bs_ref[:, col])

    def gating(g_ref, z_ref):
        z = z_ref[...]
        return ((jnp.tanh(g_ref[...]) + 1.0) * (jnp.tanh(z) + 1.0) * z).astype(F32)

    y_mla = gating(gm_ref, zm_ref) * a_ref[...].astype(F32)
    y_sgu = gating(gs_ref, zs_ref) * u_ref[...].astype(F32) * mix_ref[...]
    twice_merged = (y_mla + y_sgu).astype(BF16)
    y = jnp.dot(twice_merged, wo_bf_ref[...], preferred_element_type=F32)
    o_ref[...] = _rms(x_ref[...] + (0.5 * gate_ref[...]) * y, fg_ref[...])


def _outproj(attn, proj, x2, gate, sg, ws, bs_full, wo, fg):
    tm = 256
    s = x2.shape[0]
    row = lambda i: (i, 0)
    big = lambda k: pl.BlockSpec((tm, D_MODEL), lambda i, k=k: (i, k))
    return pl.pallas_call(
        functools.partial(_out_kernel, tm=tm),
        grid=(s // tm,),
        in_specs=[pl.BlockSpec((tm, D_MODEL), row),
                  big(0), big(1), big(2), big(3), big(4), big(5),
                  pl.BlockSpec((tm, D_MODEL), row), _resident(gate.shape), _resident(sg.shape),
                  _resident(ws.shape), _resident(bs_full.shape), _resident(wo.shape),
                  _resident(fg.shape)],
        out_specs=pl.BlockSpec((tm, D_MODEL), row),
        out_shape=jax.ShapeDtypeStruct((s, D_MODEL), F32),
        scratch_shapes=[pltpu.VMEM((tm, D_MODEL), F32), pltpu.VMEM((D_MODEL, D_MODEL), BF16)],
        compiler_params=pltpu.CompilerParams(
            dimension_semantics=("arbitrary",), vmem_limit_bytes=VMEM_LIMIT),
        name="outproj",
    )(attn, proj, proj, proj, proj, proj, proj, x2, gate, sg, ws, bs_full, wo, fg)


def _rope_tables(positions):
    n_freq = ROPE // 2
    per_row = LANE // n_freq
    inv_freq = 1.0 / (ROPE_THETA ** (jnp.arange(0, ROPE, 2, dtype=F32) / ROPE))
    ang = (positions.astype(F32).reshape(-1, per_row, 1) * inv_freq).reshape(-1, LANE)
    cos = jnp.tile(jnp.cos(ang).reshape(-1, n_freq), (1, per_row))
    sin = jnp.tile(jnp.sin(ang).reshape(-1, n_freq), (1, per_row))
    first = (jnp.arange(LANE) % ROPE) < n_freq
    return cos, jnp.where(first, -sin, 0.0), jnp.where(first, 0.0, sin)


def kernel(x, c, positions, attn_norm_g, w_ada, b_ada, w_in, q_norm_g, w_uq, kv_norm_g, w_ukv,
           sgu_norm_g, w_spatial, b_spatial, w_out, final_norm_g):
    b, s, d = x.shape
    assert (b, s, d) == (1, SEQ, D_MODEL) and attn_norm_g.shape[0] == 1
    x2 = x.reshape(s, d)

    mod = _adaln(jnp.broadcast_to(c, (8, d)), w_ada[0], b_ada[0].reshape(1, 3 * d))[0:1]
    shift, scale, gate = mod[:, :d], mod[:, d:2 * d], mod[:, 2 * d:]

    w_t = jnp.swapaxes(w_in[0], 0, 1)
    wqn, wqr, wk, wvt = _regroup_up_weights(w_uq[0], w_ukv[0])
    h, lat = _latent(x2, attn_norm_g[0].reshape(1, d), scale, shift, w_t)
    q, k, kr, vt = _front(lat, q_norm_g[0].reshape(1, Q_RANK), kv_norm_g[0].reshape(1, KV_RANK),
                      wqn, wqr, wk, wvt, *_rope_tables(positions[0]))

    proj = _inproj(h, w_t)
    attn = _flash(q, k, kr, vt)

    bs_full = jnp.repeat(b_spatial[0].T, CHUNK, axis=1)
    out = _outproj(attn, proj, x2, gate, sgu_norm_g[0].reshape(1, d), w_spatial[0], bs_full,
                   w_out[0], final_norm_g.reshape(1, d))
    return out.reshape(b, s, d)
```
